```python
import math
import jax, jax.numpy as jnp
from jax import lax
import numpy as np

D_MODEL = 2048
BATCH = 16
SEQ = 2048
DEPTH = 2
DEC_BATCH = 8
DEC_SEQ = 2048
PAST_LEN = 128

POOL_WIDTH = D_MODEL // 2
POOL_WINDOWS = (2, 4, 8, 16)
N_POOL_GROUPS = 4
POOL_GROUP = POOL_WIDTH // N_POOL_GROUPS
GLA_WIDTH = D_MODEL - POOL_WIDTH
GLA_HEADS = 4
GLA_DV = GLA_WIDTH // GLA_HEADS
GLA_DK = GLA_DV // 2
GLA_KEY_WIDTH = GLA_HEADS * GLA_DK
GLA_GATE_RANK = 16
GLA_GATE_NORM = 16.0
GLA_CHUNK = 64
IN0_SPLITS = (POOL_WIDTH, GLA_KEY_WIDTH, GLA_KEY_WIDTH, GLA_WIDTH, GLA_WIDTH, GLA_GATE_RANK, GLA_GATE_RANK)
IN0_WIDTH = sum(IN0_SPLITS)
DIFF_HEADS = 8
DIFF_HEAD_DIM = D_MODEL // (2 * DIFF_HEADS)
DIFF_Q_BLOCK = 128
ROPE_THETA = 10000.0
N_EXPERTS = 32
TOP_K = 4
D_FF = D_MODEL
SWIGLU_LIMIT = 7.0
SWIGLU_ALPHA = 1.702
MOE_BLOCK = 256
N_EVEN = (DEPTH + 1) // 2
N_ODD = DEPTH // 2
DN_ALPHA = (2.0 * DEPTH) ** 0.25
DN_BETA = (8.0 * DEPTH) ** -0.25
LN_EPS = 1e-5

kernel_name = 'hybrid_pool_gla_diffattn_moe_encoder'


def layer_norm(x, g, b):
    xf = x.astype(jnp.float32)
    mu = jnp.mean(xf, -1, keepdims=True)
    var = jnp.mean(jnp.square(xf - mu), -1, keepdims=True)
    return ((xf - mu) * lax.rsqrt(var + LN_EPS) * g + b).astype(x.dtype)


def multiscale_pool(u, w_pool, pool_scale):
    B, S, _ = u.shape
    uf = u.astype(jnp.float32)
    cs = jnp.concatenate([jnp.zeros((B, 1, POOL_WIDTH), jnp.float32), jnp.cumsum(uf, axis=1)], axis=1)
    pos = jnp.arange(S)
    outs = []
    for g, w in enumerate(POOL_WINDOWS):
        lo = jnp.clip(pos - w // 2, 0, S)
        hi = jnp.clip(pos + w // 2, 0, S)
        csg = cs[..., g * POOL_GROUP:(g + 1) * POOL_GROUP]
        summed = jnp.take(csg, hi, axis=1) - jnp.take(csg, lo, axis=1)
        cnt = (hi - lo).astype(jnp.float32)[None, :, None]
        outs.append(summed / cnt)
    pooled = (jnp.concatenate(outs, -1) - uf).reshape(B, S, N_POOL_GROUPS, POOL_GROUP).astype(u.dtype)
    mixed = jnp.einsum('bsgc,gcd->bsgd', pooled, w_pool).reshape(B, S, POOL_WIDTH)
    return mixed * pool_scale


def gla_direction(q, k, v, g):
    B, H, S, _ = q.shape
    nc = S // GLA_CHUNK

    def chunks(t):
        return jnp.moveaxis(t.reshape(B, H, nc, GLA_CHUNK, t.shape[-1]), 2, 0)

    mask = jnp.tril(jnp.ones((GLA_CHUNK, GLA_CHUNK), bool))

    def step(state, inp):
        qi, ki, vi, gi = inp
        b = jnp.cumsum(gi, axis=-2)
        b_last = b[..., -1:, :]
        q_t = qi * jnp.exp(b)
        k_t = ki * jnp.exp(-b)
        att = jnp.where(mask, jnp.einsum('bhik,bhjk->bhij', q_t, k_t), 0.0)
        o = jnp.einsum('bhij,bhjv->bhiv', att, vi) + jnp.einsum('bhik,bhkv->bhiv', q_t, state)
        k_dec = ki * jnp.exp(b_last - b)
        state = state * jnp.swapaxes(jnp.exp(b_last), -1, -2) + jnp.einsum('bhjk,bhjv->bhkv', k_dec, vi)
        return state, o

    state0 = jnp.zeros((B, H, GLA_DK, GLA_DV), jnp.float32)
    _, o = lax.scan(step, state0, (chunks(q), chunks(k), chunks(v), chunks(g)))
    return jnp.moveaxis(o, 0, 2).reshape(B, H, S, GLA_DV)


def gla_mixer(q, k, v, r, glo_f, glo_b, w_gate2, b_gate2, gla_norm_g):
    B, S, _ = q.shape

    def heads(t, d):
        return t.reshape(B, S, GLA_HEADS, d).transpose(0, 2, 1, 3).astype(jnp.float32)

    g_f = jax.nn.log_sigmoid((glo_f @ w_gate2[0] + b_gate2[0]).astype(jnp.float32)) / GLA_GATE_NORM
    g_b = jax.nn.log_sigmoid((glo_b @ w_gate2[1] + b_gate2[1]).astype(jnp.float32)) / GLA_GATE_NORM
    qh = heads(q, GLA_DK) * (GLA_DK ** -0.5)
    kh = heads(k, GLA_DK)
    vh = heads(v, GLA_DV)
    gf = heads(g_f, GLA_DK)
    gb = heads(g_b, GLA_DK)
    flip = lambda t: jnp.flip(t, axis=2)
    o = gla_direction(jnp.concatenate([qh, flip(qh)], 0), jnp.concatenate([kh, flip(kh)], 0),
                      jnp.concatenate([vh, flip(vh)], 0), jnp.concatenate([gf, flip(gb)], 0))
    o = o[:B] + flip(o[B:])
    mu = jnp.mean(o, -1, keepdims=True)
    var = jnp.mean(jnp.square(o - mu), -1, keepdims=True)
    o = ((o - mu) * lax.rsqrt(var + LN_EPS)).transpose(0, 2, 1, 3).reshape(B, S, GLA_WIDTH) * gla_norm_g
    return (jax.nn.silu(r.astype(jnp.float32)) * o).astype(q.dtype)


def pool_gla_mixer(x, w_in, w_pool, pool_scale, w_gate2, b_gate2, gla_norm_g, w_out):
    h = x @ w_in
    u, q, k, v, r, glo_f, glo_b = jnp.split(h, list(np.cumsum(IN0_SPLITS[:-1])), axis=-1)
    a_out = multiscale_pool(u, w_pool, pool_scale)
    b_out = gla_mixer(q, k, v, r, glo_f, glo_b, w_gate2, b_gate2, gla_norm_g)
    return jnp.concatenate([a_out.astype(x.dtype), b_out], -1) @ w_out


def rope(t, pos):
    half = t.shape[-1] // 2
    inv = ROPE_THETA ** (-jnp.arange(half, dtype=jnp.float32) / half)
    ang = pos.astype(jnp.float32)[:, None] * inv[None, :]
    cos, sin = jnp.cos(ang), jnp.sin(ang)
    t1, t2 = t[..., :half], t[..., half:]
    return jnp.concatenate([t1 * cos - t2 * sin, t1 * sin + t2 * cos], -1).astype(t.dtype)


def diff_attention_mixer(x, w_qkv, lam_q1, lam_k1, lam_q2, lam_k2, subln_g, w_out, lam_init):
    B, S, D = x.shape
    H2 = 2 * DIFF_HEADS
    q, k, v = jnp.split(x @ w_qkv, 3, axis=-1)
    pos = jnp.arange(S)
    q = rope(q.reshape(B, S, H2, DIFF_HEAD_DIM).transpose(0, 2, 1, 3), pos) * (DIFF_HEAD_DIM ** -0.5)
    k = rope(k.reshape(B, S, H2, DIFF_HEAD_DIM).transpose(0, 2, 1, 3), pos)
    v = v.reshape(B, S, DIFF_HEADS, 2 * DIFF_HEAD_DIM).transpose(0, 2, 1, 3)
    lam = (jnp.exp(jnp.sum(lam_q1 * lam_k1).astype(jnp.float32))
           - jnp.exp(jnp.sum(lam_q2 * lam_k2).astype(jnp.float32)) + lam_init)
    nb = S // DIFF_Q_BLOCK
    qb = jnp.moveaxis(q.reshape(B, H2, nb, DIFF_Q_BLOCK, DIFF_HEAD_DIM), 2, 0)

    def block(qi):
        s = jnp.einsum('bhqd,bhkd->bhqk', qi, k).astype(jnp.float32)
        p = jax.nn.softmax(s, axis=-1).reshape(B, DIFF_HEADS, 2, DIFF_Q_BLOCK, S)
        a = p[:, :, 0] - lam * p[:, :, 1]
        return jnp.einsum('bhqk,bhkv->bhqv', a.astype(v.dtype), v)

    o = jnp.moveaxis(lax.map(block, qb), 0, 2).reshape(B, DIFF_HEADS, S, 2 * DIFF_HEAD_DIM)
    of = o.astype(jnp.float32)
    of = of * lax.rsqrt(jnp.mean(jnp.square(of), -1, keepdims=True) + LN_EPS) * subln_g * (1.0 - lam_init)
    return of.transpose(0, 2, 1, 3).reshape(B, S, D).astype(x.dtype) @ w_out


def moe_ffn(x, w_router, b_router, w_gate_up, b_gate_up, w_down, b_down):
    B, S, D = x.shape
    xt = x.reshape(-1, D)
    N = xt.shape[0]
    logits = (xt @ w_router + b_router).astype(jnp.float32)
    top_val, top_idx = lax.top_k(logits, TOP_K)
    gates = jax.nn.softmax(top_val, axis=-1)
    A = N * TOP_K
    flat_e = top_idx.reshape(-1)
    flat_tok = jnp.arange(A, dtype=jnp.int32) // TOP_K
    flat_w = gates.reshape(-1)
    order = jnp.argsort(flat_e)
    e_sorted = flat_e[order]
    counts = jnp.bincount(flat_e, length=N_EXPERTS)
    padded = ((counts + MOE_BLOCK - 1) // MOE_BLOCK) * MOE_BLOCK
    start = jnp.cumsum(counts) - counts
    pend = jnp.cumsum(padded)
    pstart = pend - padded
    dest = pstart[e_sorted] + (jnp.arange(A) - start[e_sorted])
    n_blocks = -(-A // MOE_BLOCK) + N_EXPERTS
    P = n_blocks * MOE_BLOCK
    row_tok = jnp.zeros((P,), jnp.int32).at[dest].set(flat_tok[order])
    row_w = jnp.zeros((P,), jnp.float32).at[dest].set(flat_w[order])
    block_e = jnp.minimum(jnp.searchsorted(pend, jnp.arange(n_blocks) * MOE_BLOCK, side='right'), N_EXPERTS - 1)
    xb = xt[row_tok].reshape(n_blocks, MOE_BLOCK, D)

    def expert_block(args):
        xi, e = args
        h = xi @ w_gate_up[e] + b_gate_up[e]
        gate = jnp.minimum(h[..., ::2], SWIGLU_LIMIT)
        up = jnp.clip(h[..., 1::2], -SWIGLU_LIMIT, SWIGLU_LIMIT)
        glu = gate * jax.nn.sigmoid(gate * SWIGLU_ALPHA)
        return ((up + 1.0) * glu) @ w_down[e] + b_down[e]

    yb = lax.map(expert_block, (xb, block_e)).reshape(P, D)
    out = jnp.zeros((N, D), jnp.float32).at[row_tok].add(yb * row_w[:, None])
    return out.astype(x.dtype).reshape(B, S, D)


def trunk(x, e_w_in, e_w_pool, e_pool_scale, e_w_gate2, e_b_gate2, e_gla_norm_g, e_w_out,
          o_w_qkv, o_lam_q1, o_lam_k1, o_lam_q2, o_lam_k2, o_subln_g, o_w_out,
          ln1_g, ln1_b, ln2_g, ln2_b,
          moe_w_router, moe_b_router, moe_w_gate_up, moe_b_gate_up, moe_w_down, moe_b_down):
    for l in range(DEPTH):
        i = l // 2
        if l % 2 == 0:
            mix = pool_gla_mixer(x, e_w_in[i], e_w_pool[i], e_pool_scale[i], e_w_gate2[i], e_b_gate2[i],
                                 e_gla_norm_g[i], e_w_out[i])
        else:
            lam_init = 0.8 - 0.6 * math.exp(-0.3 * l)
            mix = diff_attention_mixer(x, o_w_qkv[i], o_lam_q1[i], o_lam_k1[i], o_lam_q2[i], o_lam_k2[i],
                                       o_subln_g[i], o_w_out[i], lam_init)
        x = layer_norm(DN_ALPHA * x + mix, ln1_g[l], ln1_b[l])
        ff = moe_ffn(x, moe_w_router[l], moe_b_router[l], moe_w_gate_up[l], moe_b_gate_up[l],
                     moe_w_down[l], moe_b_down[l])
        x = layer_norm(DN_ALPHA * x + ff, ln2_g[l], ln2_b[l])
    return x


def setup_inputs(seed: int = 0) -> dict:
    key = jax.random.key(seed)
    ks = jax.random.split(key, 32)
    D = D_MODEL

    def nrm(k, shape, scale):
        return jax.random.normal(k, shape, jnp.float32) * scale

    return {
        'x_prompt': nrm(ks[0], (BATCH, SEQ, D), 1.0),
        'x_sample': nrm(ks[1], (DEC_BATCH, DEC_SEQ, D), 1.0),
        'e_w_in': nrm(ks[2], (N_EVEN, D, IN0_WIDTH), D ** -0.5),
        'e_w_pool': nrm(ks[3], (N_EVEN, N_POOL_GROUPS, POOL_GROUP, POOL_GROUP), POOL_GROUP ** -0.5),
        'e_pool_scale': 1.0 + nrm(ks[4], (N_EVEN, POOL_WIDTH), 0.1),
        'e_w_gate2': nrm(ks[5], (N_EVEN, 2, GLA_GATE_RANK, GLA_KEY_WIDTH), GLA_GATE_RANK ** -0.5),
        'e_b_gate2': nrm(ks[6], (N_EVEN, 2, GLA_KEY_WIDTH), 0.1),
        'e_gla_norm_g': 1.0 + nrm(ks[7], (N_EVEN, GLA_WIDTH), 0.1),
        'e_w_out': nrm(ks[8], (N_EVEN, D, D), D ** -0.5 * DN_BETA),
        'o_w_qkv': nrm(ks[9], (N_ODD, D, 3 * D), D ** -0.5),
        'o_lam_q1': nrm(ks[10], (N_ODD, DIFF_HEAD_DIM), 0.1),
        'o_lam_k1': nrm(ks[11], (N_ODD, DIFF_HEAD_DIM), 0.1),
        'o_lam_q2': nrm(ks[12], (N_ODD, DIFF_HEAD_DIM), 0.1),
        'o_lam_k2': nrm(ks[13], (N_ODD, DIFF_HEAD_DIM), 0.1),
        'o_subln_g': 1.0 + nrm(ks[14], (N_ODD, 2 * DIFF_HEAD_DIM), 0.1),
        'o_w_out': nrm(ks[15], (N_ODD, D, D), D ** -0.5 * DN_BETA),
        'ln1_g': 1.0 + nrm(ks[16], (DEPTH, D), 0.1),
        'ln1_b': nrm(ks[17], (DEPTH, D), 0.02),
        'ln2_g': 1.0 + nrm(ks[18], (DEPTH, D), 0.1),
        'ln2_b': nrm(ks[19], (DEPTH, D), 0.02),
        'moe_w_router': nrm(ks[20], (DEPTH, D, N_EXPERTS), D ** -0.5),
        'moe_b_router': nrm(ks[21], (DEPTH, N_EXPERTS), 0.01),
        'moe_w_gate_up': nrm(ks[22], (DEPTH, N_EXPERTS, D, 2 * D_FF), D ** -0.5),
        'moe_b_gate_up': nrm(ks[23], (DEPTH, N_EXPERTS, 2 * D_FF), 0.01),
        'moe_w_down': nrm(ks[24], (DEPTH, N_EXPERTS, D_FF, D), D_FF ** -0.5 * DN_BETA),
        'moe_b_down': nrm(ks[25], (DEPTH, N_EXPERTS, D), 0.01),
    }


def reference(x_prompt, x_sample, e_w_in, e_w_pool, e_pool_scale, e_w_gate2, e_b_gate2, e_gla_norm_g, e_w_out,
              o_w_qkv, o_lam_q1, o_lam_k1, o_lam_q2, o_lam_k2, o_subln_g, o_w_out,
              ln1_g, ln1_b, ln2_g, ln2_b,
              moe_w_router, moe_b_router, moe_w_gate_up, moe_b_gate_up, moe_w_down, moe_b_down):
    y_prompt = trunk(x_prompt, e_w_in, e_w_pool, e_pool_scale, e_w_gate2, e_b_gate2, e_gla_norm_g, e_w_out,
                     o_w_qkv, o_lam_q1, o_lam_k1, o_lam_q2, o_lam_k2, o_subln_g, o_w_out,
                     ln1_g, ln1_b, ln2_g, ln2_b,
                     moe_w_router, moe_b_router, moe_w_gate_up, moe_b_gate_up, moe_w_down, moe_b_down)
    y_sample = trunk(x_sample, e_w_in, e_w_pool, e_pool_scale, e_w_gate2, e_b_gate2, e_gla_norm_g, e_w_out,
                     o_w_qkv, o_lam_q1, o_lam_k1, o_lam_q2, o_lam_k2, o_subln_g, o_w_out,
                     ln1_g, ln1_b, ln2_g, ln2_b,
                     moe_w_router, moe_b_router, moe_w_gate_up, moe_b_gate_up, moe_w_down, moe_b_down)
    return (y_prompt, y_sample)
```

```python
import functools
import math

import jax
import jax.numpy as jnp
from jax import lax
from jax.experimental import pallas as pl
from jax.experimental.pallas import tpu as pltpu

F32 = jnp.float32
BF16 = jnp.bfloat16
I32 = jnp.int32
U32 = jnp.uint32

POOL_WINDOWS = (2, 4, 8, 16)
GLA_HEADS = 4
GLA_GATE_RANK = 16
GLA_GATE_NORM = 16.0
GLA_CHUNK = 64
DIFF_HEADS = 8
ROPE_THETA = 10000.0
TOP_K = 4
SWIGLU_LIMIT = 7.0
SWIGLU_ALPHA = 1.702
LN_EPS = 1e-5

V7X_VMEM_BYTES = 64 * 2 ** 20
VMEM_LIMIT = V7X_VMEM_BYTES - 8 * 2 ** 20
LANES = 128


def _tile(n, pref):
    t = min(n, pref)
    while n % t:
        t //= 2
    return t


def _cparams(sem):
    return pltpu.CompilerParams(dimension_semantics=sem, vmem_limit_bytes=VMEM_LIMIT)


def _nt_dot(a, b):
    return lax.dot_general(a, b, (((1,), (1,)), ((), ())), preferred_element_type=F32)


def _layer_norm(y, g, b):
    mu = jnp.mean(y, axis=-1, keepdims=True)
    yc = y - mu
    var = jnp.mean(yc * yc, axis=-1, keepdims=True)
    return yc * lax.rsqrt(var + LN_EPS) * g + b


def _mm_kernel(x_ref, w_ref, o_ref, xb_ref):
    @pl.when(pl.program_id(1) == 0)
    def _():
        xb_ref[...] = x_ref[...].astype(BF16)

    o_ref[...] = jnp.dot(xb_ref[...], w_ref[...], preferred_element_type=F32).astype(o_ref.dtype)


def _matmul(x, w, out_dtype, name, tm_pref=512, tn_pref=1024):
    m, k = x.shape
    n = w.shape[1]
    tm, tn = _tile(m, tm_pref), _tile(n, tn_pref)
    return pl.pallas_call(
        _mm_kernel,
        grid=(m // tm, n // tn),
        in_specs=[pl.BlockSpec((tm, k), lambda i, j: (i, 0)),
                  pl.BlockSpec((k, tn), lambda i, j: (0, j))],
        out_specs=pl.BlockSpec((tm, tn), lambda i, j: (i, j)),
        out_shape=jax.ShapeDtypeStruct((m, n), out_dtype),
        scratch_shapes=[pltpu.VMEM((tm, k), BF16)],
        compiler_params=_cparams(("parallel", "arbitrary")),
        name=name,
    )(x, w)


def _rope_table_kernel(cos_ref, sin_ref, *, half, theta):
    rows, hd = cos_ref.shape
    pos = (lax.broadcasted_iota(I32, (rows, hd), 0) + pl.program_id(0) * rows).astype(F32)
    lane = lax.broadcasted_iota(I32, (rows, hd), 1)
    fi = jnp.where(lane >= half, lane - half, lane).astype(F32)
    inv = jnp.exp(fi * (-math.log(theta) / half))
    ang = pos * inv
    cos_ref[...] = jnp.cos(ang)
    s = jnp.sin(ang)
    sin_ref[...] = jnp.where(lane < half, -s, s)


def _rope_tables(seq, hd):
    rows = _tile(seq, 256)
    return pl.pallas_call(
        functools.partial(_rope_table_kernel, half=hd // 2, theta=ROPE_THETA),
        grid=(seq // rows,),
        out_specs=[pl.BlockSpec((rows, hd), lambda i: (i, 0))] * 2,
        out_shape=[jax.ShapeDtypeStruct((seq, hd), F32)] * 2,
        compiler_params=_cparams(("parallel",)),
        name="rope_tables",
    )()


def _qkv_kernel(x_ref, w_ref, cos_ref, sin_ref, o_ref, xb_ref, *, n_q_tiles, n_rope_tiles, hd, q_scale):
    j = pl.program_id(1)

    @pl.when(j == 0)
    def _():
        xb_ref[...] = x_ref[...].astype(BF16)

    acc = jnp.dot(xb_ref[...], w_ref[...], preferred_element_type=F32)
    tn = acc.shape[1]

    def rope_store(scale):
        cos = cos_ref[...]
        sin = sin_ref[...]
        for h in range(tn // hd):
            t = acc[:, h * hd:(h + 1) * hd]
            r = t * cos + pltpu.roll(t, hd // 2, axis=1) * sin
            if scale is not None:
                r = r * scale
            o_ref[:, h * hd:(h + 1) * hd] = r.astype(o_ref.dtype)

    @pl.when(j < n_q_tiles)
    def _():
        rope_store(q_scale)

    @pl.when(jnp.logical_and(j >= n_q_tiles, j < n_rope_tiles))
    def _():
        rope_store(None)

    @pl.when(j >= n_rope_tiles)
    def _():
        o_ref[...] = acc.astype(o_ref.dtype)


def _qkv_proj(x, w, cos, sin, seq, d_model, hd):
    m, k = x.shape
    n = w.shape[1]
    tm, tn = _tile(seq, 512), _tile(d_model, 512)
    spb = seq // tm
    kern = functools.partial(_qkv_kernel, n_q_tiles=d_model // tn, n_rope_tiles=2 * d_model // tn,
                             hd=hd, q_scale=hd ** -0.5)
    return pl.pallas_call(
        kern,
        grid=(m // tm, n // tn),
        in_specs=[pl.BlockSpec((tm, k), lambda i, j: (i, 0)),
                  pl.BlockSpec((k, tn), lambda i, j: (0, j)),
                  pl.BlockSpec((tm, hd), lambda i, j: (i % spb, 0)),
                  pl.BlockSpec((tm, hd), lambda i, j: (i % spb, 0))],
        out_specs=pl.BlockSpec((tm, tn), lambda i, j: (i, j)),
        out_shape=jax.ShapeDtypeStruct((m, n), BF16),
        scratch_shapes=[pltpu.VMEM((tm, k), BF16)],
        compiler_params=_cparams(("parallel", "arbitrary")),
        name="qkv_rope",
    )(x, w, cos, sin)


POOL_HALO = 8


def _pool_kernel(u_ref, w_ref, sc_ref, o_ref, pad_ref, *, windows, gw, seq, rc):
    assert max(windows) // 2 <= POOL_HALO
    pad_ref[0:POOL_HALO, :] = jnp.zeros((POOL_HALO, gw), F32)
    pad_ref[POOL_HALO + seq:POOL_HALO + seq + POOL_HALO, :] = jnp.zeros((POOL_HALO, gw), F32)
    for g, w in enumerate(windows):
        hw = w // 2
        pad_ref[POOL_HALO:POOL_HALO + seq, :] = u_ref[0, :, g * gw:(g + 1) * gw]
        for c in range(seq // rc):
            base = POOL_HALO + c * rc
            acc = pad_ref[base - hw:base - hw + rc, :]
            for off in range(-hw + 1, hw):
                acc = acc + pad_ref[base + off:base + off + rc, :]
            pos = lax.broadcasted_iota(I32, (rc, gw), 0) + c * rc
            cnt = (jnp.minimum(pos + hw, seq) - jnp.maximum(pos - hw, 0)).astype(F32)
            pooled = acc / cnt - pad_ref[base:base + rc, :]
            mixed = jnp.dot(pooled.astype(BF16), w_ref[g], preferred_element_type=F32)
            mixed = mixed * sc_ref[:, g * gw:(g + 1) * gw]
            o_ref[0, c * rc:(c + 1) * rc, g * gw:(g + 1) * gw] = mixed.astype(o_ref.dtype)


def _pool_mixer(h_main, w_pool, pool_scale, pool_width):
    b, seq, _ = h_main.shape
    g = w_pool.shape[0]
    gw = pool_width // g
    rc = _tile(seq, 256)
    kern = functools.partial(_pool_kernel, windows=POOL_WINDOWS, gw=gw, seq=seq, rc=rc)
    return pl.pallas_call(
        kern,
        grid=(b,),
        in_specs=[pl.BlockSpec((1, seq, pool_width), lambda i: (i, 0, 0)),
                  pl.BlockSpec((g, gw, gw), lambda i: (0, 0, 0)),
                  pl.BlockSpec((1, pool_width), lambda i: (0, 0))],
        out_specs=pl.BlockSpec((1, seq, pool_width), lambda i: (i, 0, 0)),
        out_shape=jax.ShapeDtypeStruct((b, seq, pool_width), BF16),
        scratch_shapes=[pltpu.VMEM((seq + 2 * POOL_HALO, gw), F32)],
        compiler_params=_cparams(("parallel",)),
        name="pool_mixer",
    )(h_main, w_pool, pool_scale)


def _log_sigmoid(x):
    return jnp.minimum(x, 0.0) - jnp.log(1.0 + jnp.exp(-jnp.abs(x)))


def _gla_kernel(q_ref, k_ref, v_ref, r_ref, gl_ref, wgf_ref, wgb_ref, bgf_ref, bgb_ref, gn_ref, o_ref,
                qm_f, km_f, kd_f, qx_f, kx_f, qm_b, km_b, kd_b, qx_b, kx_b,
                vt_ref, dec_f, dec_b, of_ref, ob_ref, sf_ref, sb_ref, *, seq, dk, dv, chunk, rb):
    pair = 2 * chunk
    n_pairs = seq // pair
    q_scale = dk ** -0.5

    def prep(blk, carry):
        r0 = pl.multiple_of(blk * rb, rb)
        rows = pl.ds(r0, rb)
        q = q_ref[0, rows, :] * q_scale
        k = k_ref[0, rows, :]
        gl = gl_ref[0, rows, :].astype(BF16)
        row = lax.broadcasted_iota(I32, (rb, dk), 0)
        rc = jnp.bitwise_and(row, chunk - 1)
        second = jnp.bitwise_and(row, chunk) != 0

        def scans(pre):
            g = _log_sigmoid(pre) / GLA_GATE_NORM
            pfx = g
            sfx = g
            sh = 1
            while sh < chunk:
                pfx = pfx + jnp.where(rc >= sh, pltpu.roll(pfx, sh, axis=0), 0.0)
                sfx = sfx + jnp.where(rc + sh < chunk, pltpu.roll(sfx, rb - sh, axis=0), 0.0)
                sh *= 2
            tot = pfx + sfx - g
            return pfx, sfx, tot

        def emit(bcum, tot, q_extra_rows, qm, km, kd, qx, kx, dec):
            tot_prev = pltpu.roll(tot, chunk, axis=0)
            tot_next = pltpu.roll(tot, rb - chunk, axis=0)
            other = jnp.where(second, tot_prev, tot_next)
            qm[rows, :] = (q * jnp.exp(bcum)).astype(BF16)
            km[rows, :] = (k * jnp.exp(-bcum)).astype(BF16)
            kd[rows, :] = (k * jnp.exp(tot - bcum)).astype(BF16)
            qx[rows, :] = (q * jnp.exp(bcum + jnp.where(q_extra_rows, other, 0.0))).astype(BF16)
            kx[rows, :] = (k * jnp.exp(tot - bcum + jnp.where(q_extra_rows, 0.0, other))).astype(BF16)
            d = jnp.exp(tot + other)
            for p in range(rb // pair):
                dec[blk * (rb // pair) + p] = d[p * pair:p * pair + 1, :]

        pre_f = jnp.dot(gl, wgf_ref[...], preferred_element_type=F32) + bgf_ref[...]
        pfx, _, tot = scans(pre_f)
        emit(pfx, tot, second, qm_f, km_f, kd_f, qx_f, kx_f, dec_f)
        pre_b = jnp.dot(gl, wgb_ref[...], preferred_element_type=F32) + bgb_ref[...]
        _, sfx, tot = scans(pre_b)
        emit(sfx, tot, jnp.logical_not(second), qm_b, km_b, kd_b, qx_b, kx_b, dec_b)
        return carry

    lax.fori_loop(0, seq // rb, prep, 0)

    for p in range(n_pairs):
        vt_ref[p] = v_ref[0, p * pair:(p + 1) * pair, :].T.astype(BF16)

    ri = lax.broadcasted_iota(I32, (pair, pair), 0)
    ci = lax.broadcasted_iota(I32, (pair, pair), 1)
    same = (ri >= chunk) == (ci >= chunk)
    m1_f = jnp.logical_and(same, ci <= ri)
    m2_f = jnp.logical_and(ri >= chunk, ci < chunk)
    m1_b = jnp.logical_and(same, ci >= ri)
    m2_b = jnp.logical_and(ri < chunk, ci >= chunk)

    sf_ref[...] = jnp.zeros((dv, dk), F32)
    sb_ref[...] = jnp.zeros((dv, dk), F32)

    def step(p, qm, km, kd, qx, kx, dec, s_ref, out, m1, m2):
        rows = pl.ds(pl.multiple_of(p * pair, pair), pair)
        a1 = _nt_dot(qm[rows, :], km[rows, :])
        a2 = _nt_dot(qm[rows, :], kd[rows, :])
        att = jnp.where(m1, a1, jnp.where(m2, a2, 0.0)).astype(BF16)
        vb = v_ref[0, rows, :].astype(BF16)
        st = s_ref[...]
        o = jnp.dot(att, vb, preferred_element_type=F32) + _nt_dot(qx[rows, :], st.astype(BF16))
        out[rows, :] = o
        s_ref[...] = st * dec[p] + jnp.dot(vt_ref[p], kx[rows, :], preferred_element_type=F32)

    def scan(i, carry):
        step(i, qm_f, km_f, kd_f, qx_f, kx_f, dec_f, sf_ref, of_ref, m1_f, m2_f)
        step(n_pairs - 1 - i, qm_b, km_b, kd_b, qx_b, kx_b, dec_b, sb_ref, ob_ref, m1_b, m2_b)
        return carry

    lax.fori_loop(0, n_pairs, scan, 0)

    def finish(blk, carry):
        rows = pl.ds(pl.multiple_of(blk * rb, rb), rb)
        o = of_ref[rows, :] + ob_ref[rows, :]
        mu = jnp.mean(o, axis=-1, keepdims=True)
        oc = o - mu
        var = jnp.mean(oc * oc, axis=-1, keepdims=True)
        on = oc * lax.rsqrt(var + LN_EPS) * gn_ref[...]
        r = r_ref[0, rows, :]
        silu = r / (1.0 + jnp.exp(-r))
        o_ref[0, rows, :] = (silu * on).astype(o_ref.dtype)
        return carry

    lax.fori_loop(0, seq // rb, finish, 0)


def _gla_mixer(h_main, h_gate, wgf, wgb, bgf, bgb, gla_norm_g, pool_width, heads):
    b, seq, _ = h_main.shape
    gla_width = gla_norm_g.shape[1]
    dv = gla_width // heads
    dk = dv // 2
    keyw = heads * dk
    chunk = GLA_CHUNK
    pair = 2 * chunk
    assert seq % pair == 0 and chunk & (chunk - 1) == 0
    rb = _tile(seq, 256)
    assert rb % pair == 0
    q0, k0 = pool_width // dk, (pool_width + keyw) // dk
    v0, r0 = (pool_width + 2 * keyw) // dv, (pool_width + 2 * keyw + gla_width) // dv
    assert pool_width % dk == 0 and (pool_width + 2 * keyw) % dv == 0
    gk = h_gate.shape[2]
    kern = functools.partial(_gla_kernel, seq=seq, dk=dk, dv=dv, chunk=chunk, rb=rb)
    row_dk = pltpu.VMEM((seq, dk), BF16)
    return pl.pallas_call(
        kern,
        grid=(b, heads),
        in_specs=[pl.BlockSpec((1, seq, dk), lambda i, h: (i, 0, q0 + h)),
                  pl.BlockSpec((1, seq, dk), lambda i, h: (i, 0, k0 + h)),
                  pl.BlockSpec((1, seq, dv), lambda i, h: (i, 0, v0 + h)),
                  pl.BlockSpec((1, seq, dv), lambda i, h: (i, 0, r0 + h)),
                  pl.BlockSpec((1, seq, gk), lambda i, h: (i, 0, 0)),
                  pl.BlockSpec((gk, dk), lambda i, h: (0, h)),
                  pl.BlockSpec((gk, dk), lambda i, h: (0, h)),
                  pl.BlockSpec((1, dk), lambda i, h: (0, h)),
                  pl.BlockSpec((1, dk), lambda i, h: (0, h)),
                  pl.BlockSpec((1, dv), lambda i, h: (0, h))],
        out_specs=pl.BlockSpec((1, seq, dv), lambda i, h: (i, 0, h)),
        out_shape=jax.ShapeDtypeStruct((b, seq, gla_width), BF16),
        scratch_shapes=[row_dk] * 10 + [
            pltpu.VMEM((seq // pair, dv, pair), BF16),
            pltpu.VMEM((seq // pair, 1, dk), F32),
            pltpu.VMEM((seq // pair, 1, dk), F32),
            pltpu.VMEM((seq, dv), F32),
            pltpu.VMEM((seq, dv), F32),
            pltpu.VMEM((dv, dk), F32),
            pltpu.VMEM((dv, dk), F32)],
        compiler_params=_cparams(("parallel", "arbitrary")),
        name="gla_mixer",
    )(h_main, h_main, h_main, h_main, h_gate, wgf, wgb, bgf, bgb, gla_norm_g)


def _diff_attn_kernel(q1_ref, q2_ref, k1_ref, k2_ref, v_ref, lq1_ref, lk1_ref, lq2_ref, lk2_ref, sg_ref,
                      o_ref, *, lam_init):
    lam = (jnp.exp(jnp.sum(lq1_ref[...] * lk1_ref[...], axis=-1, keepdims=True))
           - jnp.exp(jnp.sum(lq2_ref[...] * lk2_ref[...], axis=-1, keepdims=True)) + lam_init)
    v = v_ref[0]

    def branch(q_ref, k_ref):
        s = _nt_dot(q_ref[0], k_ref[0])
        m = jnp.max(s, axis=-1, keepdims=True)
        p = jnp.exp(s - m)
        l = jnp.sum(p, axis=-1, keepdims=True)
        return jnp.dot(p.astype(BF16), v, preferred_element_type=F32) / l

    o = branch(q1_ref, k1_ref) - lam * branch(q2_ref, k2_ref)
    rms = lax.rsqrt(jnp.mean(o * o, axis=-1, keepdims=True) + LN_EPS)
    o_ref[0] = (o * rms * sg_ref[...] * (1.0 - lam_init)).astype(o_ref.dtype)


def _diff_attention(qkv, lam_q1, lam_k1, lam_q2, lam_k2, subln_g, lam_init, d_model, heads):
    b, seq, _ = qkv.shape
    hd = d_model // (2 * heads)
    tq = _tile(seq, 512)
    k0 = d_model // hd
    v0 = 2 * d_model // (2 * hd)
    kern = functools.partial(_diff_attn_kernel, lam_init=lam_init)
    vec = pl.BlockSpec((1, hd), lambda i, h, qi: (0, 0))
    return pl.pallas_call(
        kern,
        grid=(b, heads, seq // tq),
        in_specs=[pl.BlockSpec((1, tq, hd), lambda i, h, qi: (i, qi, 2 * h)),
                  pl.BlockSpec((1, tq, hd), lambda i, h, qi: (i, qi, 2 * h + 1)),
                  pl.BlockSpec((1, seq, hd), lambda i, h, qi: (i, 0, k0 + 2 * h)),
                  pl.BlockSpec((1, seq, hd), lambda i, h, qi: (i, 0, k0 + 2 * h + 1)),
                  pl.BlockSpec((1, seq, 2 * hd), lambda i, h, qi: (i, 0, v0 + h)),
                  vec, vec, vec, vec,
                  pl.BlockSpec((1, 2 * hd), lambda i, h, qi: (0, 0))],
        out_specs=pl.BlockSpec((1, tq, 2 * hd), lambda i, h, qi: (i, qi, h)),
        out_shape=jax.ShapeDtypeStruct((b, seq, d_model), BF16),
        compiler_params=_cparams(("parallel", "parallel", "arbitrary")),
        name="diff_attention",
    )(qkv, qkv, qkv, qkv, qkv, lam_q1, lam_k1, lam_q2, lam_k2, subln_g)


def _proj_ln_router_kernel(*refs, n_lhs, alpha, n_exp, top_k):
    lhs = refs[:n_lhs]
    w_ref, x_ref, g_ref, b_ref, wr_ref, br_ref, tri_ref = refs[n_lhs:n_lhs + 7]
    x1_ref, idx_ref, gate_ref, rank_ref, cnt_ref = refs[n_lhs + 7:n_lhs + 12]
    run_ref = refs[n_lhs + 12]

    @pl.when(pl.program_id(0) == 0)
    def _():
        run_ref[...] = jnp.zeros(run_ref.shape, F32)

    off = 0
    acc = None
    for l in lhs:
        kl = l.shape[1]
        part = jnp.dot(l[...], w_ref[off:off + kl, :], preferred_element_type=F32)
        acc = part if acc is None else acc + part
        off += kl
    x1 = _layer_norm(alpha * x_ref[...] + acc, g_ref[...], b_ref[...])
    x1_ref[...] = x1

    tm = x1.shape[0]
    logits = _nt_dot(wr_ref[...], x1.astype(BF16)) + br_ref[...]
    eidx = lax.broadcasted_iota(I32, (n_exp, tm), 0)
    vals, idxs, hots = [], [], []
    for _ in range(top_k):
        m = jnp.max(logits, axis=0, keepdims=True)
        idx = jnp.min(jnp.where(logits == m, eidx, n_exp - 1), axis=0, keepdims=True)
        hot = eidx == idx
        vals.append(m)
        idxs.append(idx)
        hots.append(hot)
        logits = jnp.where(hot, -jnp.inf, logits)
    exps = [jnp.exp(v - vals[0]) for v in vals]
    den = exps[0]
    for e in exps[1:]:
        den = den + e
    gate_ref[...] = jnp.concatenate([e / den for e in exps], axis=0)
    idx_ref[...] = jnp.concatenate(idxs, axis=0)

    hot_all = jnp.concatenate([h.astype(BF16) for h in hots], axis=0)
    before = jnp.dot(hot_all, tri_ref[...], preferred_element_type=F32)
    run = run_ref[:, 0:1]
    ranks = []
    for kk in range(top_k):
        hf = hots[kk].astype(F32)
        ranks.append(jnp.sum(hf * (run + before[kk * n_exp:(kk + 1) * n_exp, :]), axis=0, keepdims=True))
        run = run + jnp.sum(hf, axis=1, keepdims=True)
    rank_ref[...] = jnp.concatenate(ranks, axis=0).astype(I32)
    run_ref[...] = jnp.broadcast_to(run, run_ref.shape)
    cnt_ref[...] = run_ref[...]


def _proj_ln_router(lhs_list, w, x, ln_g, ln_b, wr_t, br, tri, alpha, tm):
    n, d = x.shape
    n_exp = wr_t.shape[0]
    kern = functools.partial(_proj_ln_router_kernel, n_lhs=len(lhs_list), alpha=alpha, n_exp=n_exp, top_k=TOP_K)
    row = lambda i: (i, 0)
    fixed = lambda i: (0, 0)
    in_specs = [pl.BlockSpec((tm, l.shape[1]), row) for l in lhs_list] + [
        pl.BlockSpec(w.shape, fixed), pl.BlockSpec((tm, d), row),
        pl.BlockSpec((1, d), fixed), pl.BlockSpec((1, d), fixed),
        pl.BlockSpec((n_exp, d), fixed), pl.BlockSpec((n_exp, 1), fixed), pl.BlockSpec((tm, tm), fixed)]
    tok = lambda i: (0, i)
    return pl.pallas_call(
        kern,
        grid=(n // tm,),
        in_specs=in_specs,
        out_specs=[pl.BlockSpec((tm, d), row), pl.BlockSpec((TOP_K, tm), tok), pl.BlockSpec((TOP_K, tm), tok),
                   pl.BlockSpec((TOP_K, tm), tok), pl.BlockSpec((n_exp, LANES), fixed)],
        out_shape=[jax.ShapeDtypeStruct((n, d), F32), jax.ShapeDtypeStruct((TOP_K, n), I32),
                   jax.ShapeDtypeStruct((TOP_K, n), F32), jax.ShapeDtypeStruct((TOP_K, n), I32),
                   jax.ShapeDtypeStruct((n_exp, LANES), F32)],
        scratch_shapes=[pltpu.VMEM((n_exp, LANES), F32)],
        compiler_params=_cparams(("arbitrary",)),
        name="proj_ln_router",
    )(*lhs_list, w, x, ln_g, ln_b, wr_t, br, tri)


def _dispatch_kernel(dest_ref, x_ref, xs_in_ref, xs_ref, xp_ref, sem, *, top_k):
    del xs_in_ref
    tm, d = x_ref.shape
    x = x_ref[...]
    lo = lax.bitcast_convert_type(x[:, :d // 2].astype(BF16).astype(F32), U32)
    hi = lax.bitcast_convert_type(x[:, d // 2:].astype(BF16).astype(F32), U32)
    xp_ref[...] = jnp.bitwise_or(lax.shift_right_logical(lo, jnp.uint32(16)),
                                 jnp.bitwise_and(hi, jnp.uint32(0xFFFF0000)))

    for kk in range(top_k):
        def issue(t, carry, kk=kk):
            dst = xs_ref.at[pl.ds(dest_ref[kk * tm + t], 1)]
            pltpu.make_async_copy(xp_ref.at[pl.ds(t, 1)], dst, sem).start()
            return carry

        lax.fori_loop(0, tm, issue, 0, unroll=8)
    for _ in range(top_k):
        pltpu.make_async_copy(xp_ref, xs_ref.at[pl.ds(0, tm)], sem).wait()


def _dispatch(dest_flat, x1, xs_zero, tm):
    n, d = x1.shape
    kern = functools.partial(_dispatch_kernel, top_k=TOP_K)
    return pl.pallas_call(
        kern,
        grid=(n // tm,),
        in_specs=[pl.BlockSpec((TOP_K * tm,), lambda i: (i,), memory_space=pltpu.SMEM),
                  pl.BlockSpec((tm, d), lambda i: (i, 0)),
                  pl.BlockSpec(memory_space=pl.ANY)],
        out_specs=pl.BlockSpec(memory_space=pl.ANY),
        out_shape=jax.ShapeDtypeStruct(xs_zero.shape, U32),
        scratch_shapes=[pltpu.VMEM((tm, d // 2), U32), pltpu.SemaphoreType.DMA(())],
        input_output_aliases={2: 0},
        compiler_params=_cparams(("arbitrary",)),
        name="moe_dispatch",
    )(dest_flat, x1, xs_zero)


def _expert_kernel(be_ref, nv_ref, xs_ref, wg_ref, wu_ref, bg_ref, bu_ref, wd_ref, bd_ref, y_ref,
                   xb_ref, acc_ref, *, n_ff_tiles):
    del be_ref
    i, j = pl.program_id(0), pl.program_id(1)
    half = xs_ref.shape[1]

    @pl.when(i < nv_ref[0])
    def _():
        @pl.when(j == 0)
        def _():
            u = xs_ref[...]
            lo = lax.bitcast_convert_type(lax.shift_left(u, jnp.uint32(16)), F32)
            hi = lax.bitcast_convert_type(jnp.bitwise_and(u, jnp.uint32(0xFFFF0000)), F32)
            xb_ref[:, :half] = lo.astype(BF16)
            xb_ref[:, half:] = hi.astype(BF16)

        xb = xb_ref[...]
        hg = jnp.dot(xb, wg_ref[0], preferred_element_type=F32) + bg_ref[0]
        hu = jnp.dot(xb, wu_ref[0], preferred_element_type=F32) + bu_ref[0]
        gate = jnp.minimum(hg, SWIGLU_LIMIT)
        up = jnp.clip(hu, -SWIGLU_LIMIT, SWIGLU_LIMIT)
        glu = gate / (1.0 + jnp.exp(-SWIGLU_ALPHA * gate))
        part = jnp.dot(((up + 1.0) * glu).astype(BF16), wd_ref[0], preferred_element_type=F32)

        @pl.when(j == 0)
        def _():
            acc_ref[...] = part

        @pl.when(j > 0)
        def _():
            acc_ref[...] += part

        @pl.when(j == n_ff_tiles - 1)
        def _():
            y_ref[...] = acc_ref[...] + bd_ref[0]

    @pl.when(jnp.logical_and(i >= nv_ref[0], j == 0))
    def _():
        y_ref[...] = jnp.zeros(y_ref.shape, y_ref.dtype)


def _expert_ffn(block_e, n_valid, xs, wg, wu, bg, bu, wd, bd, tm, tn):
    p, half = xs.shape
    d = 2 * half
    d_ff = wg.shape[2]
    nj = d_ff // tn

    def blk(i, nv):
        return jnp.minimum(i, nv[0] - 1)

    def jj(i, j, nv):
        return jnp.where(i < nv[0], j, nj - 1)

    grid_spec = pltpu.PrefetchScalarGridSpec(
        num_scalar_prefetch=2,
        grid=(p // tm, nj),
        in_specs=[pl.BlockSpec((tm, half), lambda i, j, be, nv: (blk(i, nv), 0)),
                  pl.BlockSpec((1, d, tn), lambda i, j, be, nv: (be[blk(i, nv)], 0, jj(i, j, nv))),
                  pl.BlockSpec((1, d, tn), lambda i, j, be, nv: (be[blk(i, nv)], 0, jj(i, j, nv))),
                  pl.BlockSpec((1, 1, tn), lambda i, j, be, nv: (be[blk(i, nv)], 0, jj(i, j, nv))),
                  pl.BlockSpec((1, 1, tn), lambda i, j, be, nv: (be[blk(i, nv)], 0, jj(i, j, nv))),
                  pl.BlockSpec((1, tn, d), lambda i, j, be, nv: (be[blk(i, nv)], jj(i, j, nv), 0)),
                  pl.BlockSpec((1, 1, d), lambda i, j, be, nv: (be[blk(i, nv)], 0, 0))],
        out_specs=pl.BlockSpec((tm, d), lambda i, j, be, nv: (i, 0)),
        scratch_shapes=[pltpu.VMEM((tm, d), BF16), pltpu.VMEM((tm, d), F32)])
    return pl.pallas_call(
        functools.partial(_expert_kernel, n_ff_tiles=nj),
        grid_spec=grid_spec,
        out_shape=jax.ShapeDtypeStruct((p, d), F32),
        compiler_params=_cparams(("arbitrary", "arbitrary")),
        name="moe_experts",
    )(block_e, n_valid, xs, wg, wu, bg, bu, wd, bd)


def _combine_kernel(dest_ref, x_ref, gates_ref, g_ref, b_ref, y_ref, o_ref, buf_ref, sem, *, top_k, alpha):
    tm, d = x_ref.shape

    for kk in range(top_k):
        def issue(t, carry, kk=kk):
            src = y_ref.at[pl.ds(dest_ref[kk * tm + t], 1)]
            pltpu.make_async_copy(src, buf_ref.at[kk, pl.ds(t, 1)], sem).start()
            return carry

        lax.fori_loop(0, tm, issue, 0, unroll=8)
    for kk in range(top_k):
        pltpu.make_async_copy(y_ref.at[pl.ds(0, tm)], buf_ref.at[kk], sem).wait()
    ff = gates_ref[:, 0:1] * buf_ref[0]
    for kk in range(1, top_k):
        ff = ff + gates_ref[:, kk:kk + 1] * buf_ref[kk]
    o_ref[...] = _layer_norm(alpha * x_ref[...] + ff, g_ref[...], b_ref[...])


def _combine(dest_flat, x1, gates_t, ln_g, ln_b, y, alpha, tm):
    n, d = x1.shape
    kern = functools.partial(_combine_kernel, top_k=TOP_K, alpha=alpha)
    return pl.pallas_call(
        kern,
        grid=(n // tm,),
        in_specs=[pl.BlockSpec((TOP_K * tm,), lambda i: (i,), memory_space=pltpu.SMEM),
                  pl.BlockSpec((tm, d), lambda i: (i, 0)),
                  pl.BlockSpec((tm, TOP_K), lambda i: (i, 0)),
                  pl.BlockSpec((1, d), lambda i: (0, 0)),
                  pl.BlockSpec((1, d), lambda i: (0, 0)),
                  pl.BlockSpec(memory_space=pl.ANY)],
        out_specs=pl.BlockSpec((tm, d), lambda i: (i, 0)),
        out_shape=jax.ShapeDtypeStruct((n, d), F32),
        scratch_shapes=[pltpu.VMEM((TOP_K, tm, d), F32), pltpu.SemaphoreType.DMA(())],
        compiler_params=_cparams(("arbitrary",)),
        name="moe_combine",
    )(dest_flat, x1, gates_t, ln_g, ln_b, y)


MOE_ROW_TILE = 512
MOE_FF_TILE = 512
MOE_TOKEN_TILE = 256
ROUTER_TILE = 256


def _moe_layer(x1, idx, gates, rank, cnt, layer, ew, ln_g, ln_b, alpha):
    n, d = x1.shape
    n_exp = cnt.shape[0]
    tm_e = _tile(n * TOP_K, MOE_ROW_TILE)
    tok = _tile(n, MOE_TOKEN_TILE)
    counts = cnt[:, 0].astype(I32)
    padded = ((counts + tm_e - 1) // tm_e) * tm_e
    pend = jnp.cumsum(padded)
    pstart = pend - padded
    dest = pstart[idx] + rank
    n_blocks = (n * TOP_K) // tm_e + n_exp
    block_e = jnp.minimum(jnp.searchsorted(pend, jnp.arange(n_blocks, dtype=I32) * tm_e, side='right'),
                          n_exp - 1).astype(I32) + layer * n_exp
    n_valid = (pend[-1:] // tm_e).astype(I32)
    dest_flat = dest.reshape(TOP_K, n // tok, tok).transpose(1, 0, 2).reshape(-1)
    xs = _dispatch(dest_flat, x1, jnp.zeros((n_blocks * tm_e, d // 2), U32), tok)
    y = _expert_ffn(block_e, n_valid, xs, *ew, tm_e, _tile(ew[0].shape[2], MOE_FF_TILE))
    return _combine(dest_flat, x1, gates.T, ln_g, ln_b, y, alpha, tok)


def kernel(x_prompt, x_sample, e_w_in, e_w_pool, e_pool_scale, e_w_gate2, e_b_gate2, e_gla_norm_g, e_w_out,
           o_w_qkv, o_lam_q1, o_lam_k1, o_lam_q2, o_lam_k2, o_subln_g, o_w_out,
           ln1_g, ln1_b, ln2_g, ln2_b,
           moe_w_router, moe_b_router, moe_w_gate_up, moe_b_gate_up, moe_w_down, moe_b_down):
    depth = ln1_g.shape[0]
    alpha = (2.0 * depth) ** 0.25
    nb_p, seq, d = x_prompt.shape
    assert x_sample.shape[1:] == (seq, d)
    x = jnp.concatenate([x_prompt, x_sample], axis=0)
    bsz = x.shape[0]
    n = bsz * seq
    x = x.reshape(n, d)

    n_exp = moe_w_router.shape[2]
    d_ff = moe_w_down.shape[2]
    ew = (moe_w_gate_up[..., 0::2].astype(BF16).reshape(depth * n_exp, d, d_ff),
          moe_w_gate_up[..., 1::2].astype(BF16).reshape(depth * n_exp, d, d_ff),
          moe_b_gate_up[..., 0::2].reshape(depth * n_exp, 1, d_ff),
          moe_b_gate_up[..., 1::2].reshape(depth * n_exp, 1, d_ff),
          moe_w_down.astype(BF16).reshape(depth * n_exp, d_ff, d),
          moe_b_down.reshape(depth * n_exp, 1, d))
    wr_t = jnp.swapaxes(moe_w_router, 1, 2).astype(BF16)
    tm_r = _tile(n, ROUTER_TILE)
    tri = (jnp.arange(tm_r)[:, None] < jnp.arange(tm_r)[None, :]).astype(BF16)

    pool_width = e_w_pool.shape[1] * e_w_pool.shape[2]
    gla_width = d - pool_width
    keyw = gla_width // 2
    main_w = pool_width + 2 * keyw + 2 * gla_width
    hd = d // (2 * DIFF_HEADS)

    for l in range(depth):
        i = l // 2
        if l % 2 == 0:
            w_in = e_w_in[i]
            h_main = _matmul(x, w_in[:, :main_w].astype(BF16), F32, "in_proj").reshape(bsz, seq, main_w)
            w_gl = jnp.pad(w_in[:, main_w:], ((0, 0), (0, LANES - 2 * GLA_GATE_RANK))).astype(BF16)
            h_gate = _matmul(x, w_gl, F32, "in_proj_gates").reshape(bsz, seq, LANES)
            a_out = _pool_mixer(h_main, e_w_pool[i].astype(BF16), e_pool_scale[i][None, :], pool_width)
            wg2 = e_w_gate2[i].astype(BF16)
            zpad = jnp.zeros((LANES - 2 * GLA_GATE_RANK, keyw), BF16)
            zrank = jnp.zeros((GLA_GATE_RANK, keyw), BF16)
            wgf = jnp.concatenate([wg2[0], zrank, zpad], axis=0)
            wgb = jnp.concatenate([zrank, wg2[1], zpad], axis=0)
            b_out = _gla_mixer(h_main, h_gate, wgf, wgb, e_b_gate2[i][0][None, :], e_b_gate2[i][1][None, :],
                               e_gla_norm_g[i][None, :], pool_width, GLA_HEADS)
            lhs = [a_out.reshape(n, pool_width), b_out.reshape(n, gla_width)]
            w_out = e_w_out[i].astype(BF16)
        else:
            lam_init = 0.8 - 0.6 * math.exp(-0.3 * l)
            cos, sin = _rope_tables(seq, hd)
            qkv = _qkv_proj(x, o_w_qkv[i].astype(BF16), cos, sin, seq, d, hd).reshape(bsz, seq, 3 * d)
            attn = _diff_attention(qkv, o_lam_q1[i][None, :], o_lam_k1[i][None, :], o_lam_q2[i][None, :],
                                   o_lam_k2[i][None, :], o_subln_g[i][None, :], lam_init, d, DIFF_HEADS)
            lhs = [attn.reshape(n, d)]
            w_out = o_w_out[i].astype(BF16)
        x1, idx, gates, rank, cnt = _proj_ln_router(
            lhs, w_out, x, ln1_g[l][None, :], ln1_b[l][None, :], wr_t[l], moe_b_router[l][:, None], tri,
            alpha, tm_r)
        x = _moe_layer(x1, idx, gates, rank, cnt, l, ew, ln2_g[l][None, :], ln2_b[l][None, :], alpha)

    x = x.reshape(bsz, seq, d)
    return (x[:nb_p], x[nb_p:])
```

```python
import functools
import math

import jax
import jax.numpy as jnp
from jax import lax
from jax.experimental import pallas as pl
from jax.experimental.pallas import tpu as pltpu

F32 = jnp.float32
BF16 = jnp.bfloat16
I32 = jnp.int32
U32 = jnp.uint32

POOL_WINDOWS = (2, 4, 8, 16)
GLA_HEADS = 4
GLA_GATE_RANK = 16
GLA_GATE_NORM = 16.0
GLA_CHUNK = 64
DIFF_HEADS = 8
ROPE_THETA = 10000.0
TOP_K = 4
SWIGLU_LIMIT = 7.0
SWIGLU_ALPHA = 1.702
LN_EPS = 1e-5

V7X_VMEM_BYTES = 64 * 2 ** 20
VMEM_LIMIT = V7X_VMEM_BYTES - 8 * 2 ** 20
LANES = 128
DMA_PRIORITIES = 2


def _tile(n, pref):
    t = min(n, pref)
    while n % t:
        t //= 2
    return t


def _cparams(sem):
    return pltpu.CompilerParams(dimension_semantics=sem, vmem_limit_bytes=VMEM_LIMIT)


def _nt_dot(a, b):
    return lax.dot_general(a, b, (((1,), (1,)), ((), ())), preferred_element_type=F32)


def _layer_norm(y, g, b):
    mu = jnp.mean(y, axis=-1, keepdims=True)
    yc = y - mu
    var = jnp.mean(yc * yc, axis=-1, keepdims=True)
    return yc * lax.rsqrt(var + LN_EPS) * g + b


def _mm_kernel(x_ref, w_ref, o_ref, xb_ref):
    @pl.when(pl.program_id(1) == 0)
    def _():
        xb_ref[...] = x_ref[...].astype(BF16)

    o_ref[...] = jnp.dot(xb_ref[...], w_ref[...], preferred_element_type=F32).astype(o_ref.dtype)


def _matmul(x, w, out_dtype, name, tm_pref=512, tn_pref=1024):
    m, k = x.shape
    n = w.shape[1]
    tm, tn = _tile(m, tm_pref), _tile(n, tn_pref)
    return pl.pallas_call(
        _mm_kernel,
        grid=(m // tm, n // tn),
        in_specs=[pl.BlockSpec((tm, k), lambda i, j: (i, 0)),
                  pl.BlockSpec((k, tn), lambda i, j: (0, j))],
        out_specs=pl.BlockSpec((tm, tn), lambda i, j: (i, j)),
        out_shape=jax.ShapeDtypeStruct((m, n), out_dtype),
        scratch_shapes=[pltpu.VMEM((tm, k), BF16)],
        compiler_params=_cparams(("parallel", "arbitrary")),
        name=name,
    )(x, w)


MXU_TILE = 256


def _cast_kernel(x_ref, o_ref):
    o_ref[...] = x_ref[...].astype(o_ref.dtype)


def _cast_bf16(x):
    a, r, c = x.shape
    tr = _tile(r, 512)
    return pl.pallas_call(
        _cast_kernel,
        grid=(a, r // tr),
        in_specs=[pl.BlockSpec((1, tr, c), lambda i, j: (i, j, 0))],
        out_specs=pl.BlockSpec((1, tr, c), lambda i, j: (i, j, 0)),
        out_shape=jax.ShapeDtypeStruct(x.shape, BF16),
        compiler_params=_cparams(("parallel", "parallel")),
        name="cast_bf16",
    )(x)


def _deinterleave_kernel(w_ref, p_ref, g_ref, u_ref):
    w = w_ref[0].astype(BF16)
    hm = MXU_TILE // 2
    for c in range(w.shape[1] // MXU_TILE):
        r = jnp.dot(w[:, c * MXU_TILE:(c + 1) * MXU_TILE], p_ref[...], preferred_element_type=F32).astype(BF16)
        g_ref[0, :, c * hm:(c + 1) * hm] = r[:, :hm]
        u_ref[0, :, c * hm:(c + 1) * hm] = r[:, hm:]


def _deinterleave_bf16(w):
    a, r, c2 = w.shape
    tr, tc2 = _tile(r, 1024), _tile(c2, 1024)
    assert tc2 % MXU_TILE == 0
    rows = jnp.arange(MXU_TILE)
    perm = (jnp.arange(MXU_TILE)[None, :] == (rows // 2 + (rows % 2) * (MXU_TILE // 2))[:, None]).astype(BF16)
    out = jax.ShapeDtypeStruct((a, r, c2 // 2), BF16)
    return pl.pallas_call(
        _deinterleave_kernel,
        grid=(a, r // tr, c2 // tc2),
        in_specs=[pl.BlockSpec((1, tr, tc2), lambda i, j, k: (i, j, k)),
                  pl.BlockSpec((MXU_TILE, MXU_TILE), lambda i, j, k: (0, 0))],
        out_specs=[pl.BlockSpec((1, tr, tc2 // 2), lambda i, j, k: (i, j, k))] * 2,
        out_shape=[out, out],
        compiler_params=_cparams(("parallel", "parallel", "parallel")),
        name="deinterleave_bf16",
    )(w, perm)


def _rope_table_kernel(cos_ref, sin_ref, *, half, theta):
    rows, hd = cos_ref.shape
    pos = (lax.broadcasted_iota(I32, (rows, hd), 0) + pl.program_id(0) * rows).astype(F32)
    lane = lax.broadcasted_iota(I32, (rows, hd), 1)
    fi = jnp.where(lane >= half, lane - half, lane).astype(F32)
    inv = jnp.exp(fi * (-math.log(theta) / half))
    ang = pos * inv
    cos_ref[...] = jnp.cos(ang)
    s = jnp.sin(ang)
    sin_ref[...] = jnp.where(lane < half, -s, s)


def _rope_tables(seq, hd):
    rows = _tile(seq, 256)
    return pl.pallas_call(
        functools.partial(_rope_table_kernel, half=hd // 2, theta=ROPE_THETA),
        grid=(seq // rows,),
        out_specs=[pl.BlockSpec((rows, hd), lambda i: (i, 0))] * 2,
        out_shape=[jax.ShapeDtypeStruct((seq, hd), F32)] * 2,
        compiler_params=_cparams(("parallel",)),
        name="rope_tables",
    )()


def _qkv_kernel(x_ref, w_ref, cos_ref, sin_ref, o_ref, xb_ref, *, n_q_tiles, n_rope_tiles, hd, q_scale):
    j = pl.program_id(1)

    @pl.when(j == 0)
    def _():
        xb_ref[...] = x_ref[...].astype(BF16)

    acc = jnp.dot(xb_ref[...], w_ref[...], preferred_element_type=F32)
    tn = acc.shape[1]

    rot = jnp.where(j < n_rope_tiles, jnp.where(j < n_q_tiles, q_scale, 1.0), 0.0)
    cos = cos_ref[...] * rot + jnp.where(j < n_rope_tiles, 0.0, 1.0)
    sin = sin_ref[...] * rot
    for h in range(tn // hd):
        t = acc[:, h * hd:(h + 1) * hd]
        r = t * cos + pltpu.roll(t, hd // 2, axis=1) * sin
        o_ref[:, h * hd:(h + 1) * hd] = r.astype(o_ref.dtype)


def _qkv_proj(x, w, cos, sin, seq, d_model, hd):
    m, k = x.shape
    n = w.shape[1]
    tm, tn = _tile(seq, 512), _tile(d_model, 512)
    spb = seq // tm
    kern = functools.partial(_qkv_kernel, n_q_tiles=d_model // tn, n_rope_tiles=2 * d_model // tn,
                             hd=hd, q_scale=hd ** -0.5)
    return pl.pallas_call(
        kern,
        grid=(m // tm, n // tn),
        in_specs=[pl.BlockSpec((tm, k), lambda i, j: (i, 0)),
                  pl.BlockSpec((k, tn), lambda i, j: (0, j)),
                  pl.BlockSpec((tm, hd), lambda i, j: (i % spb, 0)),
                  pl.BlockSpec((tm, hd), lambda i, j: (i % spb, 0))],
        out_specs=pl.BlockSpec((tm, tn), lambda i, j: (i, j)),
        out_shape=jax.ShapeDtypeStruct((m, n), BF16),
        scratch_shapes=[pltpu.VMEM((tm, k), BF16)],
        compiler_params=_cparams(("parallel", "arbitrary")),
        name="qkv_rope",
    )(x, w, cos, sin)


POOL_HALO = 8


def _pool_kernel(u_ref, w_ref, sc_ref, o_ref, pad_ref, *, windows, gw, seq, rc):
    assert max(windows) // 2 <= POOL_HALO
    pad_ref[0:POOL_HALO, :] = jnp.zeros((POOL_HALO, gw), F32)
    pad_ref[POOL_HALO + seq:POOL_HALO + seq + POOL_HALO, :] = jnp.zeros((POOL_HALO, gw), F32)
    for g, w in enumerate(windows):
        hw = w // 2
        pad_ref[POOL_HALO:POOL_HALO + seq, :] = u_ref[0, :, g * gw:(g + 1) * gw]
        for c in range(seq // rc):
            base = POOL_HALO + c * rc
            acc = pad_ref[base - hw:base - hw + rc, :]
            for off in range(-hw + 1, hw):
                acc = acc + pad_ref[base + off:base + off + rc, :]
            pos = lax.broadcasted_iota(I32, (rc, gw), 0) + c * rc
            cnt = (jnp.minimum(pos + hw, seq) - jnp.maximum(pos - hw, 0)).astype(F32)
            pooled = acc / cnt - pad_ref[base:base + rc, :]
            mixed = jnp.dot(pooled.astype(BF16), w_ref[g], preferred_element_type=F32)
            mixed = mixed * sc_ref[:, g * gw:(g + 1) * gw]
            o_ref[0, c * rc:(c + 1) * rc, g * gw:(g + 1) * gw] = mixed.astype(o_ref.dtype)


def _pool_mixer(h_main, w_pool, pool_scale, pool_width):
    b, seq, _ = h_main.shape
    g = w_pool.shape[0]
    gw = pool_width // g
    rc = _tile(seq, 256)
    kern = functools.partial(_pool_kernel, windows=POOL_WINDOWS, gw=gw, seq=seq, rc=rc)
    return pl.pallas_call(
        kern,
        grid=(b,),
        in_specs=[pl.BlockSpec((1, seq, pool_width), lambda i: (i, 0, 0)),
                  pl.BlockSpec((g, gw, gw), lambda i: (0, 0, 0)),
                  pl.BlockSpec((1, pool_width), lambda i: (0, 0))],
        out_specs=pl.BlockSpec((1, seq, pool_width), lambda i: (i, 0, 0)),
        out_shape=jax.ShapeDtypeStruct((b, seq, pool_width), BF16),
        scratch_shapes=[pltpu.VMEM((seq + 2 * POOL_HALO, gw), F32)],
        compiler_params=_cparams(("parallel",)),
        name="pool_mixer",
    )(h_main, w_pool, pool_scale)


def _log_sigmoid(x):
    return jnp.minimum(x, 0.0) - jnp.log(1.0 + jnp.exp(-jnp.abs(x)))


def _gla_kernel(q_ref, k_ref, v_ref, r_ref, gl_ref, wgf_ref, wgb_ref, bgf_ref, bgb_ref, gn_ref, o_ref,
                qm_f, km_f, kd_f, qx_f, kx_f, qm_b, km_b, kd_b, qx_b, kx_b,
                vt_ref, dec_f, dec_b, of_ref, ob_ref, sf_ref, sb_ref, *, seq, dk, dv, chunk, rb):
    pair = 2 * chunk
    n_pairs = seq // pair
    q_scale = dk ** -0.5

    def prep(blk, carry):
        r0 = pl.multiple_of(blk * rb, rb)
        rows = pl.ds(r0, rb)
        q = q_ref[0, rows, :] * q_scale
        k = k_ref[0, rows, :]
        gl = gl_ref[0, rows, :].astype(BF16)
        row = lax.broadcasted_iota(I32, (rb, dk), 0)
        rc = jnp.bitwise_and(row, chunk - 1)
        second = jnp.bitwise_and(row, chunk) != 0

        def scans(pre):
            g = _log_sigmoid(pre) / GLA_GATE_NORM
            pfx = g
            sfx = g
            sh = 1
            while sh < chunk:
                pfx = pfx + jnp.where(rc >= sh, pltpu.roll(pfx, sh, axis=0), 0.0)
                sfx = sfx + jnp.where(rc + sh < chunk, pltpu.roll(sfx, rb - sh, axis=0), 0.0)
                sh *= 2
            tot = pfx + sfx - g
            return pfx, sfx, tot

        def emit(bcum, tot, q_extra_rows, qm, km, kd, qx, kx, dec):
            tot_prev = pltpu.roll(tot, chunk, axis=0)
            tot_next = pltpu.roll(tot, rb - chunk, axis=0)
            other = jnp.where(second, tot_prev, tot_next)
            qm[rows, :] = (q * jnp.exp(bcum)).astype(BF16)
            km[rows, :] = (k * jnp.exp(-bcum)).astype(BF16)
            kd[rows, :] = (k * jnp.exp(tot - bcum)).astype(BF16)
            qx[rows, :] = (q * jnp.exp(bcum + jnp.where(q_extra_rows, other, 0.0))).astype(BF16)
            kx[rows, :] = (k * jnp.exp(tot - bcum + jnp.where(q_extra_rows, 0.0, other))).astype(BF16)
            d = jnp.exp(tot + other)
            for p in range(rb // pair):
                dec[blk * (rb // pair) + p] = d[p * pair:p * pair + 1, :]

        pre_f = jnp.dot(gl, wgf_ref[...], preferred_element_type=F32) + bgf_ref[...]
        pfx, _, tot = scans(pre_f)
        emit(pfx, tot, second, qm_f, km_f, kd_f, qx_f, kx_f, dec_f)
        pre_b = jnp.dot(gl, wgb_ref[...], preferred_element_type=F32) + bgb_ref[...]
        _, sfx, tot = scans(pre_b)
        emit(sfx, tot, jnp.logical_not(second), qm_b, km_b, kd_b, qx_b, kx_b, dec_b)
        return carry

    lax.fori_loop(0, seq // rb, prep, 0)

    for p in range(n_pairs):
        vt_ref[p] = v_ref[0, p * pair:(p + 1) * pair, :].T.astype(BF16)

    ri = lax.broadcasted_iota(I32, (pair, pair), 0)
    ci = lax.broadcasted_iota(I32, (pair, pair), 1)
    same = (ri >= chunk) == (ci >= chunk)
    m1_f = jnp.logical_and(same, ci <= ri)
    m2_f = jnp.logical_and(ri >= chunk, ci < chunk)
    m1_b = jnp.logical_and(same, ci >= ri)
    m2_b = jnp.logical_and(ri < chunk, ci >= chunk)

    sf_ref[...] = jnp.zeros((dv, dk), F32)
    sb_ref[...] = jnp.zeros((dv, dk), F32)

    def step(p, qm, km, kd, qx, kx, dec, s_ref, out, m1, m2):
        rows = pl.ds(pl.multiple_of(p * pair, pair), pair)
        a1 = _nt_dot(qm[rows, :], km[rows, :])
        a2 = _nt_dot(qm[rows, :], kd[rows, :])
        att = jnp.where(m1, a1, jnp.where(m2, a2, 0.0)).astype(BF16)
        vb = v_ref[0, rows, :].astype(BF16)
        st = s_ref[...]
        o = jnp.dot(att, vb, preferred_element_type=F32) + _nt_dot(qx[rows, :], st.astype(BF16))
        out[rows, :] = o
        s_ref[...] = st * dec[p] + jnp.dot(vt_ref[p], kx[rows, :], preferred_element_type=F32)

    def scan(i, carry):
        step(i, qm_f, km_f, kd_f, qx_f, kx_f, dec_f, sf_ref, of_ref, m1_f, m2_f)
        step(n_pairs - 1 - i, qm_b, km_b, kd_b, qx_b, kx_b, dec_b, sb_ref, ob_ref, m1_b, m2_b)
        return carry

    lax.fori_loop(0, n_pairs, scan, 0)

    def finish(blk, carry):
        rows = pl.ds(pl.multiple_of(blk * rb, rb), rb)
        o = of_ref[rows, :] + ob_ref[rows, :]
        mu = jnp.mean(o, axis=-1, keepdims=True)
        oc = o - mu
        var = jnp.mean(oc * oc, axis=-1, keepdims=True)
        on = oc * lax.rsqrt(var + LN_EPS) * gn_ref[...]
        r = r_ref[0, rows, :]
        silu = r / (1.0 + jnp.exp(-r))
        o_ref[0, rows, :] = (silu * on).astype(o_ref.dtype)
        return carry

    lax.fori_loop(0, seq // rb, finish, 0)


def _gla_mixer(h_main, h_gate, wgf, wgb, bgf, bgb, gla_norm_g, pool_width, heads):
    b, seq, _ = h_main.shape
    gla_width = gla_norm_g.shape[1]
    dv = gla_width // heads
    dk = dv // 2
    keyw = heads * dk
    chunk = GLA_CHUNK
    pair = 2 * chunk
    assert seq % pair == 0 and chunk & (chunk - 1) == 0
    rb = _tile(seq, 256)
    assert rb % pair == 0
    q0, k0 = pool_width // dk, (pool_width + keyw) // dk
    v0, r0 = (pool_width + 2 * keyw) // dv, (pool_width + 2 * keyw + gla_width) // dv
    assert pool_width % dk == 0 and (pool_width + 2 * keyw) % dv == 0
    gk = h_gate.shape[2]
    kern = functools.partial(_gla_kernel, seq=seq, dk=dk, dv=dv, chunk=chunk, rb=rb)
    row_dk = pltpu.VMEM((seq, dk), BF16)
    return pl.pallas_call(
        kern,
        grid=(b, heads),
        in_specs=[pl.BlockSpec((1, seq, dk), lambda i, h: (i, 0, q0 + h)),
                  pl.BlockSpec((1, seq, dk), lambda i, h: (i, 0, k0 + h)),
                  pl.BlockSpec((1, seq, dv), lambda i, h: (i, 0, v0 + h)),
                  pl.BlockSpec((1, seq, dv), lambda i, h: (i, 0, r0 + h)),
                  pl.BlockSpec((1, seq, gk), lambda i, h: (i, 0, 0)),
                  pl.BlockSpec((gk, dk), lambda i, h: (0, h)),
                  pl.BlockSpec((gk, dk), lambda i, h: (0, h)),
                  pl.BlockSpec((1, dk), lambda i, h: (0, h)),
                  pl.BlockSpec((1, dk), lambda i, h: (0, h)),
                  pl.BlockSpec((1, dv), lambda i, h: (0, h))],
        out_specs=pl.BlockSpec((1, seq, dv), lambda i, h: (i, 0, h)),
        out_shape=jax.ShapeDtypeStruct((b, seq, gla_width), BF16),
        scratch_shapes=[row_dk] * 10 + [
            pltpu.VMEM((seq // pair, dv, pair), BF16),
            pltpu.VMEM((seq // pair, 1, dk), F32),
            pltpu.VMEM((seq // pair, 1, dk), F32),
            pltpu.VMEM((seq, dv), F32),
            pltpu.VMEM((seq, dv), F32),
            pltpu.VMEM((dv, dk), F32),
            pltpu.VMEM((dv, dk), F32)],
        compiler_params=_cparams(("parallel", "arbitrary")),
        name="gla_mixer",
    )(h_main, h_main, h_main, h_main, h_gate, wgf, wgb, bgf, bgb, gla_norm_g)


def _diff_attn_kernel(q1_ref, q2_ref, k1_ref, k2_ref, v_ref, lq1_ref, lk1_ref, lq2_ref, lk2_ref, sg_ref,
                      o_ref, *, lam_init):
    lam = (jnp.exp(jnp.sum(lq1_ref[...] * lk1_ref[...], axis=-1, keepdims=True))
           - jnp.exp(jnp.sum(lq2_ref[...] * lk2_ref[...], axis=-1, keepdims=True)) + lam_init)
    v = v_ref[0]

    def branch(q_ref, k_ref):
        s = _nt_dot(q_ref[0], k_ref[0])
        m = jnp.max(s, axis=-1, keepdims=True)
        p = jnp.exp(s - m)
        l = jnp.sum(p, axis=-1, keepdims=True)
        return jnp.dot(p.astype(BF16), v, preferred_element_type=F32) / l

    o = branch(q1_ref, k1_ref) - lam * branch(q2_ref, k2_ref)
    rms = lax.rsqrt(jnp.mean(o * o, axis=-1, keepdims=True) + LN_EPS)
    o_ref[0] = (o * rms * sg_ref[...] * (1.0 - lam_init)).astype(o_ref.dtype)


def _diff_attention(qkv, lam_q1, lam_k1, lam_q2, lam_k2, subln_g, lam_init, d_model, heads):
    b, seq, _ = qkv.shape
    hd = d_model // (2 * heads)
    tq = _tile(seq, 512)
    k0 = d_model // hd
    v0 = 2 * d_model // (2 * hd)
    kern = functools.partial(_diff_attn_kernel, lam_init=lam_init)
    vec = pl.BlockSpec((1, hd), lambda i, h, qi: (0, 0))
    return pl.pallas_call(
        kern,
        grid=(b, heads, seq // tq),
        in_specs=[pl.BlockSpec((1, tq, hd), lambda i, h, qi: (i, qi, 2 * h)),
                  pl.BlockSpec((1, tq, hd), lambda i, h, qi: (i, qi, 2 * h + 1)),
                  pl.BlockSpec((1, seq, hd), lambda i, h, qi: (i, 0, k0 + 2 * h)),
                  pl.BlockSpec((1, seq, hd), lambda i, h, qi: (i, 0, k0 + 2 * h + 1)),
                  pl.BlockSpec((1, seq, 2 * hd), lambda i, h, qi: (i, 0, v0 + h)),
                  vec, vec, vec, vec,
                  pl.BlockSpec((1, 2 * hd), lambda i, h, qi: (0, 0))],
        out_specs=pl.BlockSpec((1, tq, 2 * hd), lambda i, h, qi: (i, qi, h)),
        out_shape=jax.ShapeDtypeStruct((b, seq, d_model), BF16),
        compiler_params=_cparams(("parallel", "parallel", "arbitrary")),
        name="diff_attention",
    )(qkv, qkv, qkv, qkv, qkv, lam_q1, lam_k1, lam_q2, lam_k2, subln_g)


def _proj_ln_router_kernel(*refs, n_lhs, alpha, n_exp, top_k):
    lhs = refs[:n_lhs]
    w_ref, x_ref, g_ref, b_ref, wr_ref, br_ref, tri_ref = refs[n_lhs:n_lhs + 7]
    x1_ref, idx_ref, gate_ref, rank_ref, cnt_ref = refs[n_lhs + 7:n_lhs + 12]
    run_ref = refs[n_lhs + 12]

    @pl.when(pl.program_id(0) == 0)
    def _():
        run_ref[...] = jnp.zeros(run_ref.shape, F32)

    off = 0
    acc = None
    for l in lhs:
        kl = l.shape[1]
        part = jnp.dot(l[...], w_ref[off:off + kl, :], preferred_element_type=F32)
        acc = part if acc is None else acc + part
        off += kl
    x1 = _layer_norm(alpha * x_ref[...] + acc, g_ref[...], b_ref[...])
    x1_ref[...] = x1

    tm = x1.shape[0]
    logits = _nt_dot(wr_ref[...], x1.astype(BF16)) + br_ref[...]
    eidx = lax.broadcasted_iota(I32, (n_exp, tm), 0)
    vals, idxs, hots = [], [], []
    for _ in range(top_k):
        m = jnp.max(logits, axis=0, keepdims=True)
        idx = jnp.min(jnp.where(logits == m, eidx, n_exp - 1), axis=0, keepdims=True)
        hot = eidx == idx
        vals.append(m)
        idxs.append(idx)
        hots.append(hot)
        logits = jnp.where(hot, -jnp.inf, logits)
    exps = [jnp.exp(v - vals[0]) for v in vals]
    den = exps[0]
    for e in exps[1:]:
        den = den + e
    gate_ref[...] = jnp.concatenate([e / den for e in exps], axis=0)
    idx_ref[...] = jnp.concatenate(idxs, axis=0)

    hot_all = jnp.concatenate([h.astype(BF16) for h in hots], axis=0)
    before = jnp.dot(hot_all, tri_ref[...], preferred_element_type=F32)
    run = run_ref[:, 0:1]
    ranks = []
    for kk in range(top_k):
        hf = hots[kk].astype(F32)
        ranks.append(jnp.sum(hf * (run + before[kk * n_exp:(kk + 1) * n_exp, :]), axis=0, keepdims=True))
        run = run + jnp.sum(hf, axis=1, keepdims=True)
    rank_ref[...] = jnp.concatenate(ranks, axis=0).astype(I32)
    run_ref[...] = jnp.broadcast_to(run, run_ref.shape)
    cnt_ref[...] = run_ref[...]


def _proj_ln_router(lhs_list, w, x, ln_g, ln_b, wr_t, br, tri, alpha, tm):
    n, d = x.shape
    n_exp = wr_t.shape[0]
    kern = functools.partial(_proj_ln_router_kernel, n_lhs=len(lhs_list), alpha=alpha, n_exp=n_exp, top_k=TOP_K)
    row = lambda i: (i, 0)
    fixed = lambda i: (0, 0)
    in_specs = [pl.BlockSpec((tm, l.shape[1]), row) for l in lhs_list] + [
        pl.BlockSpec(w.shape, fixed), pl.BlockSpec((tm, d), row),
        pl.BlockSpec((1, d), fixed), pl.BlockSpec((1, d), fixed),
        pl.BlockSpec((n_exp, d), fixed), pl.BlockSpec((n_exp, 1), fixed), pl.BlockSpec((tm, tm), fixed)]
    tok = lambda i: (0, i)
    return pl.pallas_call(
        kern,
        grid=(n // tm,),
        in_specs=in_specs,
        out_specs=[pl.BlockSpec((tm, d), row), pl.BlockSpec((TOP_K, tm), tok), pl.BlockSpec((TOP_K, tm), tok),
                   pl.BlockSpec((TOP_K, tm), tok), pl.BlockSpec((n_exp, LANES), fixed)],
        out_shape=[jax.ShapeDtypeStruct((n, d), F32), jax.ShapeDtypeStruct((TOP_K, n), I32),
                   jax.ShapeDtypeStruct((TOP_K, n), F32), jax.ShapeDtypeStruct((TOP_K, n), I32),
                   jax.ShapeDtypeStruct((n_exp, LANES), F32)],
        scratch_shapes=[pltpu.VMEM((n_exp, LANES), F32)],
        compiler_params=_cparams(("arbitrary",)),
        name="proj_ln_router",
    )(*lhs_list, w, x, ln_g, ln_b, wr_t, br, tri)


def _dispatch_kernel(dest_ref, x_ref, xs_in_ref, xs_ref, xp_ref, sem, *, top_k):
    del xs_in_ref
    tm, d = x_ref.shape
    x = x_ref[...]
    lo = lax.bitcast_convert_type(x[:, :d // 2].astype(BF16).astype(F32), U32)
    hi = lax.bitcast_convert_type(x[:, d // 2:].astype(BF16).astype(F32), U32)
    xp_ref[...] = jnp.bitwise_or(lax.shift_right_logical(lo, jnp.uint32(16)),
                                 jnp.bitwise_and(hi, jnp.uint32(0xFFFF0000)))

    for kk in range(top_k):
        def issue(t2, carry, kk=kk):
            for prio in range(DMA_PRIORITIES):
                t = DMA_PRIORITIES * t2 + prio
                dst = xs_ref.at[pl.ds(dest_ref[kk * tm + t], 1)]
                pltpu.make_async_copy(xp_ref.at[pl.ds(t, 1)], dst, sem).start(priority=prio)
            return carry

        lax.fori_loop(0, tm // DMA_PRIORITIES, issue, 0, unroll=4)
    for _ in range(top_k):
        pltpu.make_async_copy(xp_ref, xs_ref.at[pl.ds(0, tm)], sem).wait()


def _dispatch(dest_flat, x1, xs_zero, tm):
    n, d = x1.shape
    kern = functools.partial(_dispatch_kernel, top_k=TOP_K)
    return pl.pallas_call(
        kern,
        grid=(n // tm,),
        in_specs=[pl.BlockSpec((TOP_K * tm,), lambda i: (i,), memory_space=pltpu.SMEM),
                  pl.BlockSpec((tm, d), lambda i: (i, 0)),
                  pl.BlockSpec(memory_space=pl.ANY)],
        out_specs=pl.BlockSpec(memory_space=pl.ANY),
        out_shape=jax.ShapeDtypeStruct(xs_zero.shape, U32),
        scratch_shapes=[pltpu.VMEM((tm, d // 2), U32), pltpu.SemaphoreType.DMA(())],
        input_output_aliases={2: 0},
        compiler_params=_cparams(("arbitrary",)),
        name="moe_dispatch",
    )(dest_flat, x1, xs_zero)


def _expert_kernel(be_ref, nv_ref, xs_ref, wg_ref, wu_ref, bg_ref, bu_ref, wd_ref, bd_ref, y_ref,
                   xb_ref):
    del be_ref
    i, j = pl.program_id(0), pl.program_id(1)
    half = xs_ref.shape[1]

    @pl.when(j == 0)
    def _():
        u = xs_ref[...]
        lo = lax.bitcast_convert_type(lax.shift_left(u, jnp.uint32(16)), F32)
        hi = lax.bitcast_convert_type(jnp.bitwise_and(u, jnp.uint32(0xFFFF0000)), F32)
        xb_ref[:, :half] = lo.astype(BF16)
        xb_ref[:, half:] = hi.astype(BF16)
        y_ref[...] = jnp.broadcast_to(bd_ref[0], y_ref.shape)

    @pl.when(i < nv_ref[0])
    def _():
        xb = xb_ref[...]
        hg = jnp.dot(xb, wg_ref[0], preferred_element_type=F32) + bg_ref[0]
        hu = jnp.dot(xb, wu_ref[0], preferred_element_type=F32) + bu_ref[0]
        gate = jnp.minimum(hg, SWIGLU_LIMIT)
        up = jnp.clip(hu, -SWIGLU_LIMIT, SWIGLU_LIMIT)
        glu = gate / (1.0 + jnp.exp(-SWIGLU_ALPHA * gate))
        y_ref[...] += jnp.dot(((up + 1.0) * glu).astype(BF16), wd_ref[0], preferred_element_type=F32)


def _expert_ffn(block_e, n_valid, xs, wg, wu, bg, bu, wd, bd, tm, tn):
    p, half = xs.shape
    d = 2 * half
    d_ff = wg.shape[2]
    nj = d_ff // tn

    def blk(i, nv):
        return jnp.minimum(i, nv[0] - 1)

    def jj(i, j, nv):
        return jnp.where(i < nv[0], j, nj - 1)

    grid_spec = pltpu.PrefetchScalarGridSpec(
        num_scalar_prefetch=2,
        grid=(p // tm, nj),
        in_specs=[pl.BlockSpec((tm, half), lambda i, j, be, nv: (blk(i, nv), 0)),
                  pl.BlockSpec((1, d, tn), lambda i, j, be, nv: (be[blk(i, nv)], 0, jj(i, j, nv))),
                  pl.BlockSpec((1, d, tn), lambda i, j, be, nv: (be[blk(i, nv)], 0, jj(i, j, nv))),
                  pl.BlockSpec((1, 1, tn), lambda i, j, be, nv: (be[blk(i, nv)], 0, jj(i, j, nv))),
                  pl.BlockSpec((1, 1, tn), lambda i, j, be, nv: (be[blk(i, nv)], 0, jj(i, j, nv))),
                  pl.BlockSpec((1, tn, d), lambda i, j, be, nv: (be[blk(i, nv)], jj(i, j, nv), 0)),
                  pl.BlockSpec((1, 1, d), lambda i, j, be, nv: (be[blk(i, nv)], 0, 0))],
        out_specs=pl.BlockSpec((tm, d), lambda i, j, be, nv: (i, 0)),
        scratch_shapes=[pltpu.VMEM((tm, d), BF16)])
    return pl.pallas_call(
        _expert_kernel,
        grid_spec=grid_spec,
        out_shape=jax.ShapeDtypeStruct((p, d), F32),
        compiler_params=_cparams(("arbitrary", "arbitrary")),
        name="moe_experts",
    )(block_e, n_valid, xs, wg, wu, bg, bu, wd, bd)


def _combine_kernel(dest_ref, x_ref, gates_ref, g_ref, b_ref, y_ref, o_ref, buf_ref, sem, *, top_k, alpha):
    tm, d = x_ref.shape

    for kk in range(top_k):
        def issue(t2, carry, kk=kk):
            for prio in range(DMA_PRIORITIES):
                t = DMA_PRIORITIES * t2 + prio
                src = y_ref.at[pl.ds(dest_ref[kk * tm + t], 1)]
                pltpu.make_async_copy(src, buf_ref.at[kk, pl.ds(t, 1)], sem).start(priority=prio)
            return carry

        lax.fori_loop(0, tm // DMA_PRIORITIES, issue, 0, unroll=4)
    for kk in range(top_k):
        pltpu.make_async_copy(y_ref.at[pl.ds(0, tm)], buf_ref.at[kk], sem).wait()
    ff = gates_ref[:, 0:1] * buf_ref[0]
    for kk in range(1, top_k):
        ff = ff + gates_ref[:, kk:kk + 1] * buf_ref[kk]
    o_ref[...] = _layer_norm(alpha * x_ref[...] + ff, g_ref[...], b_ref[...])


def _combine(dest_flat, x1, gates_t, ln_g, ln_b, y, alpha, tm):
    n, d = x1.shape
    kern = functools.partial(_combine_kernel, top_k=TOP_K, alpha=alpha)
    return pl.pallas_call(
        kern,
        grid=(n // tm,),
        in_specs=[pl.BlockSpec((TOP_K * tm,), lambda i: (i,), memory_space=pltpu.SMEM),
                  pl.BlockSpec((tm, d), lambda i: (i, 0)),
                  pl.BlockSpec((tm, TOP_K), lambda i: (i, 0)),
                  pl.BlockSpec((1, d), lambda i: (0, 0)),
                  pl.BlockSpec((1, d), lambda i: (0, 0)),
                  pl.BlockSpec(memory_space=pl.ANY)],
        out_specs=pl.BlockSpec((tm, d), lambda i: (i, 0)),
        out_shape=jax.ShapeDtypeStruct((n, d), F32),
        scratch_shapes=[pltpu.VMEM((TOP_K, tm, d), F32), pltpu.SemaphoreType.DMA(())],
        compiler_params=_cparams(("arbitrary",)),
        name="moe_combine",
    )(dest_flat, x1, gates_t, ln_g, ln_b, y)


MOE_ROW_TILE = 512
MOE_FF_TILE = 1024
MOE_TOKEN_TILE = 256
ROUTER_TILE = 256


def _moe_layer(x1, idx, gates, rank, cnt, layer, ew, ln_g, ln_b, alpha):
    n, d = x1.shape
    n_exp = cnt.shape[0]
    tm_e = _tile(n * TOP_K, MOE_ROW_TILE)
    tok = _tile(n, MOE_TOKEN_TILE)
    counts = cnt[:, 0].astype(I32)
    padded = ((counts + tm_e - 1) // tm_e) * tm_e
    pend = jnp.cumsum(padded)
    pstart = pend - padded
    dest = rank
    for e in range(n_exp):
        dest = dest + jnp.where(idx == e, pstart[e], 0)
    n_blocks = (n * TOP_K) // tm_e + n_exp
    starts = jnp.arange(n_blocks, dtype=I32) * tm_e
    block_e = jnp.minimum(jnp.sum((starts[:, None] >= pend[None, :]).astype(I32), axis=1), n_exp - 1) + layer * n_exp
    n_valid = (pend[-1:] // tm_e).astype(I32)
    dest_flat = dest.reshape(TOP_K, n // tok, tok).transpose(1, 0, 2).reshape(-1)
    xs = _dispatch(dest_flat, x1, jnp.zeros((n_blocks * tm_e, d // 2), U32), tok)
    y = _expert_ffn(block_e, n_valid, xs, *ew, tm_e, _tile(ew[0].shape[2], MOE_FF_TILE))
    return _combine(dest_flat, x1, gates.T, ln_g, ln_b, y, alpha, tok)


def kernel(x_prompt, x_sample, e_w_in, e_w_pool, e_pool_scale, e_w_gate2, e_b_gate2, e_gla_norm_g, e_w_out,
           o_w_qkv, o_lam_q1, o_lam_k1, o_lam_q2, o_lam_k2, o_subln_g, o_w_out,
           ln1_g, ln1_b, ln2_g, ln2_b,
           moe_w_router, moe_b_router, moe_w_gate_up, moe_b_gate_up, moe_w_down, moe_b_down):
    depth = ln1_g.shape[0]
    alpha = (2.0 * depth) ** 0.25
    nb_p, seq, d = x_prompt.shape
    assert x_sample.shape[1:] == (seq, d)
    x = jnp.concatenate([x_prompt, x_sample], axis=0)
    bsz = x.shape[0]
    n = bsz * seq
    x = x.reshape(n, d)

    n_exp = moe_w_router.shape[2]
    d_ff = moe_w_down.shape[2]
    w_gate, w_up = _deinterleave_bf16(moe_w_gate_up.reshape(depth * n_exp, d, 2 * d_ff))
    ew = (w_gate, w_up,
          moe_b_gate_up[..., 0::2].reshape(depth * n_exp, 1, d_ff),
          moe_b_gate_up[..., 1::2].reshape(depth * n_exp, 1, d_ff),
          _cast_bf16(moe_w_down.reshape(depth * n_exp, d_ff, d)),
          moe_b_down.reshape(depth * n_exp, 1, d))
    wr_t = jnp.swapaxes(moe_w_router, 1, 2).astype(BF16)
    tm_r = _tile(n, ROUTER_TILE)
    tri = (jnp.arange(tm_r)[:, None] < jnp.arange(tm_r)[None, :]).astype(BF16)

    pool_width = e_w_pool.shape[1] * e_w_pool.shape[2]
    gla_width = d - pool_width
    keyw = gla_width // 2
    main_w = pool_width + 2 * keyw + 2 * gla_width
    hd = d // (2 * DIFF_HEADS)

    for l in range(depth):
        i = l // 2
        if l % 2 == 0:
            w_in = e_w_in[i]
            h_main = _matmul(x, w_in[:, :main_w].astype(BF16), F32, "in_proj").reshape(bsz, seq, main_w)
            w_gl = jnp.pad(w_in[:, main_w:], ((0, 0), (0, LANES - 2 * GLA_GATE_RANK))).astype(BF16)
            h_gate = _matmul(x, w_gl, F32, "in_proj_gates").reshape(bsz, seq, LANES)
            a_out = _pool_mixer(h_main, e_w_pool[i].astype(BF16), e_pool_scale[i][None, :], pool_width)
            wg2 = e_w_gate2[i].astype(BF16)
            zpad = jnp.zeros((LANES - 2 * GLA_GATE_RANK, keyw), BF16)
            zrank = jnp.zeros((GLA_GATE_RANK, keyw), BF16)
            wgf = jnp.concatenate([wg2[0], zrank, zpad], axis=0)
            wgb = jnp.concatenate([zrank, wg2[1], zpad], axis=0)
            b_out = _gla_mixer(h_main, h_gate, wgf, wgb, e_b_gate2[i][0][None, :], e_b_gate2[i][1][None, :],
                               e_gla_norm_g[i][None, :], pool_width, GLA_HEADS)
            lhs = [a_out.reshape(n, pool_width), b_out.reshape(n, gla_width)]
            w_out = e_w_out[i].astype(BF16)
        else:
            lam_init = 0.8 - 0.6 * math.exp(-0.3 * l)
            cos, sin = _rope_tables(seq, hd)
            qkv = _qkv_proj(x, o_w_qkv[i].astype(BF16), cos, sin, seq, d, hd).reshape(bsz, seq, 3 * d)
            attn = _diff_attention(qkv, o_lam_q1[i][None, :], o_lam_k1[i][None, :], o_lam_q2[i][None, :],
                                   o_lam_k2[i][None, :], o_subln_g[i][None, :], lam_init, d, DIFF_HEADS)
            lhs = [attn.reshape(n, d)]
            w_out = o_w_out[i].astype(BF16)
        x1, idx, gates, rank, cnt = _proj_ln_router(
            lhs, w_out, x, ln1_g[l][None, :], ln1_b[l][None, :], wr_t[l], moe_b_router[l][:, None], tri,
            alpha, tm_r)
        x = _moe_layer(x1, idx, gates, rank, cnt, l, ew, ln2_g[l][None, :], ln2_b[l][None, :], alpha)

    x = x.reshape(bsz, seq, d)
    return (x[:nb_p], x[nb_p:])
```

```python
import functools
import math

import jax
import jax.numpy as jnp
from jax import lax
from jax.experimental import pallas as pl
from jax.experimental.pallas import tpu as pltpu

F32 = jnp.float32
BF16 = jnp.bfloat16
I32 = jnp.int32
U32 = jnp.uint32

POOL_WINDOWS = (2, 4, 8, 16)
GLA_HEADS = 4
GLA_GATE_RANK = 16
GLA_GATE_NORM = 16.0
GLA_CHUNK = 64
DIFF_HEADS = 8
ROPE_THETA = 10000.0
TOP_K = 4
SWIGLU_LIMIT = 7.0
SWIGLU_ALPHA = 1.702
LN_EPS = 1e-5

V7X_VMEM_BYTES = 64 * 2 ** 20
VMEM_LIMIT = V7X_VMEM_BYTES - 8 * 2 ** 20
LANES = 128
SUBLANES = 8
SUBLANE_SHIFT = 3
DMA_PRIORITIES = 2


def _tile(n, pref):
    t = min(n, pref)
    while n % t:
        t //= 2
    return t


def _cparams(sem):
    return pltpu.CompilerParams(dimension_semantics=sem, vmem_limit_bytes=VMEM_LIMIT)


def _nt_dot(a, b):
    return lax.dot_general(a, b, (((1,), (1,)), ((), ())), preferred_element_type=F32)


def _layer_norm(y, g, b):
    mu = jnp.mean(y, axis=-1, keepdims=True)
    yc = y - mu
    var = jnp.mean(yc * yc, axis=-1, keepdims=True)
    return yc * lax.rsqrt(var + LN_EPS) * g + b


def _mm_kernel(x_ref, w_ref, o_ref, xb_ref):
    @pl.when(pl.program_id(1) == 0)
    def _():
        xb_ref[...] = x_ref[...].astype(BF16)

    o_ref[...] = jnp.dot(xb_ref[...], w_ref[...], preferred_element_type=F32).astype(o_ref.dtype)


def _matmul(x, w, out_dtype, name, tm_pref=512, tn_pref=1024):
    m, k = x.shape
    n = w.shape[1]
    tm, tn = _tile(m, tm_pref), _tile(n, tn_pref)
    return pl.pallas_call(
        _mm_kernel,
        grid=(m // tm, n // tn),
        in_specs=[pl.BlockSpec((tm, k), lambda i, j: (i, 0)),
                  pl.BlockSpec((k, tn), lambda i, j: (0, j))],
        out_specs=pl.BlockSpec((tm, tn), lambda i, j: (i, j)),
        out_shape=jax.ShapeDtypeStruct((m, n), out_dtype),
        scratch_shapes=[pltpu.VMEM((tm, k), BF16)],
        compiler_params=_cparams(("parallel", "arbitrary")),
        name=name,
    )(x, w)


MXU_TILE = 256


def _cast_kernel(x_ref, o_ref):
    o_ref[...] = x_ref[...].astype(o_ref.dtype)


def _cast_bf16(x):
    a, r, c = x.shape
    tr = _tile(r, 512)
    return pl.pallas_call(
        _cast_kernel,
        grid=(a, r // tr),
        in_specs=[pl.BlockSpec((1, tr, c), lambda i, j: (i, j, 0))],
        out_specs=pl.BlockSpec((1, tr, c), lambda i, j: (i, j, 0)),
        out_shape=jax.ShapeDtypeStruct(x.shape, BF16),
        compiler_params=_cparams(("parallel", "parallel")),
        name="cast_bf16",
    )(x)


def _deinterleave_kernel(w_ref, p_ref, g_ref, u_ref):
    w = w_ref[0].astype(BF16)
    hm = MXU_TILE // 2
    for c in range(w.shape[1] // MXU_TILE):
        r = jnp.dot(w[:, c * MXU_TILE:(c + 1) * MXU_TILE], p_ref[...], preferred_element_type=F32).astype(BF16)
        g_ref[0, :, c * hm:(c + 1) * hm] = r[:, :hm]
        u_ref[0, :, c * hm:(c + 1) * hm] = r[:, hm:]


def _deinterleave_bf16(w):
    a, r, c2 = w.shape
    tr, tc2 = _tile(r, 1024), _tile(c2, 1024)
    assert tc2 % MXU_TILE == 0
    rows = jnp.arange(MXU_TILE)
    perm = (jnp.arange(MXU_TILE)[None, :] == (rows // 2 + (rows % 2) * (MXU_TILE // 2))[:, None]).astype(BF16)
    out = jax.ShapeDtypeStruct((a, r, c2 // 2), BF16)
    return pl.pallas_call(
        _deinterleave_kernel,
        grid=(a, r // tr, c2 // tc2),
        in_specs=[pl.BlockSpec((1, tr, tc2), lambda i, j, k: (i, j, k)),
                  pl.BlockSpec((MXU_TILE, MXU_TILE), lambda i, j, k: (0, 0))],
        out_specs=[pl.BlockSpec((1, tr, tc2 // 2), lambda i, j, k: (i, j, k))] * 2,
        out_shape=[out, out],
        compiler_params=_cparams(("parallel", "parallel", "parallel")),
        name="deinterleave_bf16",
    )(w, perm)


def _rope_table_kernel(cos_ref, sin_ref, *, half, theta):
    rows, hd = cos_ref.shape
    pos = (lax.broadcasted_iota(I32, (rows, hd), 0) + pl.program_id(0) * rows).astype(F32)
    lane = lax.broadcasted_iota(I32, (rows, hd), 1)
    fi = jnp.where(lane >= half, lane - half, lane).astype(F32)
    inv = jnp.exp(fi * (-math.log(theta) / half))
    ang = pos * inv
    cos_ref[...] = jnp.cos(ang)
    s = jnp.sin(ang)
    sin_ref[...] = jnp.where(lane < half, -s, s)


def _rope_tables(seq, hd):
    rows = _tile(seq, 256)
    return pl.pallas_call(
        functools.partial(_rope_table_kernel, half=hd // 2, theta=ROPE_THETA),
        grid=(seq // rows,),
        out_specs=[pl.BlockSpec((rows, hd), lambda i: (i, 0))] * 2,
        out_shape=[jax.ShapeDtypeStruct((seq, hd), F32)] * 2,
        compiler_params=_cparams(("parallel",)),
        name="rope_tables",
    )()


def _qkv_kernel(x_ref, w_ref, cos_ref, sin_ref, o_ref, xb_ref, *, n_q_tiles, n_rope_tiles, hd, q_scale):
    j = pl.program_id(1)

    @pl.when(j == 0)
    def _():
        xb_ref[...] = x_ref[...].astype(BF16)

    acc = jnp.dot(xb_ref[...], w_ref[...], preferred_element_type=F32)
    tn = acc.shape[1]

    rot = jnp.where(j < n_rope_tiles, jnp.where(j < n_q_tiles, q_scale, 1.0), 0.0)
    cos = cos_ref[...] * rot + jnp.where(j < n_rope_tiles, 0.0, 1.0)
    sin = sin_ref[...] * rot
    for h in range(tn // hd):
        t = acc[:, h * hd:(h + 1) * hd]
        r = t * cos + pltpu.roll(t, hd // 2, axis=1) * sin
        o_ref[:, h * hd:(h + 1) * hd] = r.astype(o_ref.dtype)


def _qkv_proj(x, w, cos, sin, seq, d_model, hd):
    m, k = x.shape
    n = w.shape[1]
    tm, tn = _tile(seq, 512), _tile(d_model, 512)
    spb = seq // tm
    kern = functools.partial(_qkv_kernel, n_q_tiles=d_model // tn, n_rope_tiles=2 * d_model // tn,
                             hd=hd, q_scale=hd ** -0.5)
    return pl.pallas_call(
        kern,
        grid=(m // tm, n // tn),
        in_specs=[pl.BlockSpec((tm, k), lambda i, j: (i, 0)),
                  pl.BlockSpec((k, tn), lambda i, j: (0, j)),
                  pl.BlockSpec((tm, hd), lambda i, j: (i % spb, 0)),
                  pl.BlockSpec((tm, hd), lambda i, j: (i % spb, 0))],
        out_specs=pl.BlockSpec((tm, tn), lambda i, j: (i, j)),
        out_shape=jax.ShapeDtypeStruct((m, n), BF16),
        scratch_shapes=[pltpu.VMEM((tm, k), BF16)],
        compiler_params=_cparams(("parallel", "arbitrary")),
        name="qkv_rope",
    )(x, w, cos, sin)


POOL_HALO = 8


def _pool_kernel(u_ref, w_ref, sc_ref, o_ref, pad_ref, *, windows, gw, seq, rc):
    assert max(windows) // 2 <= POOL_HALO
    pad_ref[0:POOL_HALO, :] = jnp.zeros((POOL_HALO, gw), F32)
    pad_ref[POOL_HALO + seq:POOL_HALO + seq + POOL_HALO, :] = jnp.zeros((POOL_HALO, gw), F32)
    for g, w in enumerate(windows):
        hw = w // 2
        pad_ref[POOL_HALO:POOL_HALO + seq, :] = u_ref[0, :, g * gw:(g + 1) * gw]
        for c in range(seq // rc):
            base = POOL_HALO + c * rc
            acc = pad_ref[base - hw:base - hw + rc, :]
            for off in range(-hw + 1, hw):
                acc = acc + pad_ref[base + off:base + off + rc, :]
            pos = lax.broadcasted_iota(I32, (rc, gw), 0) + c * rc
            cnt = (jnp.minimum(pos + hw, seq) - jnp.maximum(pos - hw, 0)).astype(F32)
            pooled = acc / cnt - pad_ref[base:base + rc, :]
            mixed = jnp.dot(pooled.astype(BF16), w_ref[g], preferred_element_type=F32)
            mixed = mixed * sc_ref[:, g * gw:(g + 1) * gw]
            o_ref[0, c * rc:(c + 1) * rc, g * gw:(g + 1) * gw] = mixed.astype(o_ref.dtype)


def _pool_mixer(h_main, w_pool, pool_scale, pool_width):
    b, seq, _ = h_main.shape
    g = w_pool.shape[0]
    gw = pool_width // g
    rc = _tile(seq, 256)
    kern = functools.partial(_pool_kernel, windows=POOL_WINDOWS, gw=gw, seq=seq, rc=rc)
    return pl.pallas_call(
        kern,
        grid=(b,),
        in_specs=[pl.BlockSpec((1, seq, pool_width), lambda i: (i, 0, 0)),
                  pl.BlockSpec((g, gw, gw), lambda i: (0, 0, 0)),
                  pl.BlockSpec((1, pool_width), lambda i: (0, 0))],
        out_specs=pl.BlockSpec((1, seq, pool_width), lambda i: (i, 0, 0)),
        out_shape=jax.ShapeDtypeStruct((b, seq, pool_width), BF16),
        scratch_shapes=[pltpu.VMEM((seq + 2 * POOL_HALO, gw), F32)],
        compiler_params=_cparams(("parallel",)),
        name="pool_mixer",
    )(h_main, w_pool, pool_scale)


def _log_sigmoid(x):
    return jnp.minimum(x, 0.0) - jnp.log(1.0 + jnp.exp(-jnp.abs(x)))


def _gla_kernel(q_ref, k_ref, v_ref, r_ref, gl_ref, wgf_ref, wgb_ref, bgf_ref, bgb_ref, gn_ref, o_ref,
                qm_f, km_f, kd_f, qx_f, kx_f, qm_b, km_b, kd_b, qx_b, kx_b,
                vt_ref, dec_f, dec_b, of_ref, ob_ref, sf_ref, sb_ref, *, seq, dk, dv, chunk, rb):
    pair = 2 * chunk
    n_pairs = seq // pair
    q_scale = dk ** -0.5

    def prep(blk, carry):
        r0 = pl.multiple_of(blk * rb, rb)
        rows = pl.ds(r0, rb)
        q = q_ref[0, rows, :] * q_scale
        k = k_ref[0, rows, :]
        gl = gl_ref[0, rows, :].astype(BF16)
        row = lax.broadcasted_iota(I32, (rb, dk), 0)
        rc = jnp.bitwise_and(row, chunk - 1)
        second = jnp.bitwise_and(row, chunk) != 0

        def scans(pre):
            g = _log_sigmoid(pre) / GLA_GATE_NORM
            pfx = g
            sfx = g
            sh = 1
            while sh < chunk:
                pfx = pfx + jnp.where(rc >= sh, pltpu.roll(pfx, sh, axis=0), 0.0)
                sfx = sfx + jnp.where(rc + sh < chunk, pltpu.roll(sfx, rb - sh, axis=0), 0.0)
                sh *= 2
            tot = pfx + sfx - g
            return pfx, sfx, tot

        def emit(bcum, tot, q_extra_rows, qm, km, kd, qx, kx, dec):
            tot_prev = pltpu.roll(tot, chunk, axis=0)
            tot_next = pltpu.roll(tot, rb - chunk, axis=0)
            other = jnp.where(second, tot_prev, tot_next)
            qm[rows, :] = (q * jnp.exp(bcum)).astype(BF16)
            km[rows, :] = (k * jnp.exp(-bcum)).astype(BF16)
            kd[rows, :] = (k * jnp.exp(tot - bcum)).astype(BF16)
            qx[rows, :] = (q * jnp.exp(bcum + jnp.where(q_extra_rows, other, 0.0))).astype(BF16)
            kx[rows, :] = (k * jnp.exp(tot - bcum + jnp.where(q_extra_rows, 0.0, other))).astype(BF16)
            d = jnp.exp(tot + other)
            for p in range(rb // pair):
                dec[blk * (rb // pair) + p] = d[p * pair:p * pair + 1, :]

        pre_f = jnp.dot(gl, wgf_ref[...], preferred_element_type=F32) + bgf_ref[...]
        pfx, _, tot = scans(pre_f)
        emit(pfx, tot, second, qm_f, km_f, kd_f, qx_f, kx_f, dec_f)
        pre_b = jnp.dot(gl, wgb_ref[...], preferred_element_type=F32) + bgb_ref[...]
        _, sfx, tot = scans(pre_b)
        emit(sfx, tot, jnp.logical_not(second), qm_b, km_b, kd_b, qx_b, kx_b, dec_b)
        return carry

    lax.fori_loop(0, seq // rb, prep, 0)

    for p in range(n_pairs):
        vt_ref[p] = v_ref[0, p * pair:(p + 1) * pair, :].T.astype(BF16)

    ri = lax.broadcasted_iota(I32, (pair, pair), 0)
    ci = lax.broadcasted_iota(I32, (pair, pair), 1)
    same = (ri >= chunk) == (ci >= chunk)
    m1_f = jnp.logical_and(same, ci <= ri)
    m2_f = jnp.logical_and(ri >= chunk, ci < chunk)
    m1_b = jnp.logical_and(same, ci >= ri)
    m2_b = jnp.logical_and(ri < chunk, ci >= chunk)

    sf_ref[...] = jnp.zeros((dv, dk), F32)
    sb_ref[...] = jnp.zeros((dv, dk), F32)

    def step(p, qm, km, kd, qx, kx, dec, s_ref, out, m1, m2):
        rows = pl.ds(pl.multiple_of(p * pair, pair), pair)
        a1 = _nt_dot(qm[rows, :], km[rows, :])
        a2 = _nt_dot(qm[rows, :], kd[rows, :])
        att = jnp.where(m1, a1, jnp.where(m2, a2, 0.0)).astype(BF16)
        vb = v_ref[0, rows, :].astype(BF16)
        st = s_ref[...]
        o = jnp.dot(att, vb, preferred_element_type=F32) + _nt_dot(qx[rows, :], st.astype(BF16))
        out[rows, :] = o
        s_ref[...] = st * dec[p] + jnp.dot(vt_ref[p], kx[rows, :], preferred_element_type=F32)

    def scan(i, carry):
        step(i, qm_f, km_f, kd_f, qx_f, kx_f, dec_f, sf_ref, of_ref, m1_f, m2_f)
        step(n_pairs - 1 - i, qm_b, km_b, kd_b, qx_b, kx_b, dec_b, sb_ref, ob_ref, m1_b, m2_b)
        return carry

    lax.fori_loop(0, n_pairs, scan, 0)

    def finish(blk, carry):
        rows = pl.ds(pl.multiple_of(blk * rb, rb), rb)
        o = of_ref[rows, :] + ob_ref[rows, :]
        mu = jnp.mean(o, axis=-1, keepdims=True)
        oc = o - mu
        var = jnp.mean(oc * oc, axis=-1, keepdims=True)
        on = oc * lax.rsqrt(var + LN_EPS) * gn_ref[...]
        r = r_ref[0, rows, :]
        silu = r / (1.0 + jnp.exp(-r))
        o_ref[0, rows, :] = (silu * on).astype(o_ref.dtype)
        return carry

    lax.fori_loop(0, seq // rb, finish, 0)


def _gla_mixer(h_main, h_gate, wgf, wgb, bgf, bgb, gla_norm_g, pool_width, heads):
    b, seq, _ = h_main.shape
    gla_width = gla_norm_g.shape[1]
    dv = gla_width // heads
    dk = dv // 2
    keyw = heads * dk
    chunk = GLA_CHUNK
    pair = 2 * chunk
    assert seq % pair == 0 and chunk & (chunk - 1) == 0
    rb = _tile(seq, 256)
    assert rb % pair == 0
    q0, k0 = pool_width // dk, (pool_width + keyw) // dk
    v0, r0 = (pool_width + 2 * keyw) // dv, (pool_width + 2 * keyw + gla_width) // dv
    assert pool_width % dk == 0 and (pool_width + 2 * keyw) % dv == 0
    gk = h_gate.shape[2]
    kern = functools.partial(_gla_kernel, seq=seq, dk=dk, dv=dv, chunk=chunk, rb=rb)
    row_dk = pltpu.VMEM((seq, dk), BF16)
    return pl.pallas_call(
        kern,
        grid=(b, heads),
        in_specs=[pl.BlockSpec((1, seq, dk), lambda i, h: (i, 0, q0 + h)),
                  pl.BlockSpec((1, seq, dk), lambda i, h: (i, 0, k0 + h)),
                  pl.BlockSpec((1, seq, dv), lambda i, h: (i, 0, v0 + h)),
                  pl.BlockSpec((1, seq, dv), lambda i, h: (i, 0, r0 + h)),
                  pl.BlockSpec((1, seq, gk), lambda i, h: (i, 0, 0)),
                  pl.BlockSpec((gk, dk), lambda i, h: (0, h)),
                  pl.BlockSpec((gk, dk), lambda i, h: (0, h)),
                  pl.BlockSpec((1, dk), lambda i, h: (0, h)),
                  pl.BlockSpec((1, dk), lambda i, h: (0, h)),
                  pl.BlockSpec((1, dv), lambda i, h: (0, h))],
        out_specs=pl.BlockSpec((1, seq, dv), lambda i, h: (i, 0, h)),
        out_shape=jax.ShapeDtypeStruct((b, seq, gla_width), BF16),
        scratch_shapes=[row_dk] * 10 + [
            pltpu.VMEM((seq // pair, dv, pair), BF16),
            pltpu.VMEM((seq // pair, 1, dk), F32),
            pltpu.VMEM((seq // pair, 1, dk), F32),
            pltpu.VMEM((seq, dv), F32),
            pltpu.VMEM((seq, dv), F32),
            pltpu.VMEM((dv, dk), F32),
            pltpu.VMEM((dv, dk), F32)],
        compiler_params=_cparams(("parallel", "arbitrary")),
        name="gla_mixer",
    )(h_main, h_main, h_main, h_main, h_gate, wgf, wgb, bgf, bgb, gla_norm_g)


ATTN_Q_SUBBLOCK = 128


def _diff_attn_kernel(q1_ref, q2_ref, k1_ref, k2_ref, v_ref, lq1_ref, lk1_ref, lq2_ref, lk2_ref, sg_ref,
                      o_ref, *, lam_init, qb):
    lam = (jnp.exp(jnp.sum(lq1_ref[...] * lk1_ref[...], axis=-1, keepdims=True))
           - jnp.exp(jnp.sum(lq2_ref[...] * lk2_ref[...], axis=-1, keepdims=True)) + lam_init)
    v = v_ref[0]
    k1 = k1_ref[0]
    k2 = k2_ref[0]

    def branch(q, k):
        s = _nt_dot(q, k)
        m = jnp.max(s, axis=-1, keepdims=True)
        p = jnp.exp(s - m)
        l = jnp.sum(p, axis=-1, keepdims=True)
        return jnp.dot(p.astype(BF16), v, preferred_element_type=F32) / l

    for c in range(q1_ref.shape[1] // qb):
        rows = slice(c * qb, (c + 1) * qb)
        o = branch(q1_ref[0, rows, :], k1) - lam * branch(q2_ref[0, rows, :], k2)
        rms = lax.rsqrt(jnp.mean(o * o, axis=-1, keepdims=True) + LN_EPS)
        o_ref[0, rows, :] = (o * rms * sg_ref[...] * (1.0 - lam_init)).astype(o_ref.dtype)


def _diff_attention(qkv, lam_q1, lam_k1, lam_q2, lam_k2, subln_g, lam_init, d_model, heads):
    b, seq, _ = qkv.shape
    hd = d_model // (2 * heads)
    tq = _tile(seq, 512)
    k0 = d_model // hd
    v0 = 2 * d_model // (2 * hd)
    kern = functools.partial(_diff_attn_kernel, lam_init=lam_init, qb=_tile(tq, ATTN_Q_SUBBLOCK))
    vec = pl.BlockSpec((1, hd), lambda i, h, qi: (0, 0))
    return pl.pallas_call(
        kern,
        grid=(b, heads, seq // tq),
        in_specs=[pl.BlockSpec((1, tq, hd), lambda i, h, qi: (i, qi, 2 * h)),
                  pl.BlockSpec((1, tq, hd), lambda i, h, qi: (i, qi, 2 * h + 1)),
                  pl.BlockSpec((1, seq, hd), lambda i, h, qi: (i, 0, k0 + 2 * h)),
                  pl.BlockSpec((1, seq, hd), lambda i, h, qi: (i, 0, k0 + 2 * h + 1)),
                  pl.BlockSpec((1, seq, 2 * hd), lambda i, h, qi: (i, 0, v0 + h)),
                  vec, vec, vec, vec,
                  pl.BlockSpec((1, 2 * hd), lambda i, h, qi: (0, 0))],
        out_specs=pl.BlockSpec((1, tq, 2 * hd), lambda i, h, qi: (i, qi, h)),
        out_shape=jax.ShapeDtypeStruct((b, seq, d_model), BF16),
        compiler_params=_cparams(("parallel", "parallel", "arbitrary")),
        name="diff_attention",
    )(qkv, qkv, qkv, qkv, qkv, lam_q1, lam_k1, lam_q2, lam_k2, subln_g)


def _proj_ln_router_kernel(*refs, n_lhs, alpha, n_exp, top_k):
    lhs = refs[:n_lhs]
    w_ref, x_ref, g_ref, b_ref, wr_ref, br_ref, tri_ref = refs[n_lhs:n_lhs + 7]
    x1_ref, idx_ref, gate_ref, rank_ref, cnt_ref = refs[n_lhs + 7:n_lhs + 12]
    run_ref = refs[n_lhs + 12]

    @pl.when(pl.program_id(0) == 0)
    def _():
        run_ref[...] = jnp.zeros(run_ref.shape, F32)

    off = 0
    acc = None
    for l in lhs:
        kl = l.shape[1]
        part = jnp.dot(l[...], w_ref[off:off + kl, :], preferred_element_type=F32)
        acc = part if acc is None else acc + part
        off += kl
    x1 = _layer_norm(alpha * x_ref[...] + acc, g_ref[...], b_ref[...])
    x1_ref[...] = x1

    tm = x1.shape[0]
    logits = _nt_dot(wr_ref[...], x1.astype(BF16)) + br_ref[...]
    eidx = lax.broadcasted_iota(I32, (n_exp, tm), 0)
    vals, idxs, hots = [], [], []
    for _ in range(top_k):
        m = jnp.max(logits, axis=0, keepdims=True)
        idx = jnp.min(jnp.where(logits == m, eidx, n_exp - 1), axis=0, keepdims=True)
        hot = eidx == idx
        vals.append(m)
        idxs.append(idx)
        hots.append(hot)
        logits = jnp.where(hot, -jnp.inf, logits)
    exps = [jnp.exp(v - vals[0]) for v in vals]
    den = exps[0]
    for e in exps[1:]:
        den = den + e
    gate_ref[...] = jnp.concatenate([e / den for e in exps], axis=0)
    idx_ref[...] = jnp.concatenate(idxs, axis=0)

    hot_all = jnp.concatenate([h.astype(BF16) for h in hots], axis=0)
    before = jnp.dot(hot_all, tri_ref[...], preferred_element_type=F32)
    run = run_ref[:, 0:1]
    ranks = []
    for kk in range(top_k):
        hf = hots[kk].astype(F32)
        ranks.append(jnp.sum(hf * (run + before[kk * n_exp:(kk + 1) * n_exp, :]), axis=0, keepdims=True))
        run = run + jnp.sum(hf, axis=1, keepdims=True)
    rank_ref[...] = jnp.concatenate(ranks, axis=0).astype(I32)
    run_ref[...] = jnp.broadcast_to(run, run_ref.shape)
    cnt_ref[...] = run_ref[...]


def _proj_ln_router(lhs_list, w, x, ln_g, ln_b, wr_t, br, tri, alpha, tm):
    n, d = x.shape
    n_exp = wr_t.shape[0]
    kern = functools.partial(_proj_ln_router_kernel, n_lhs=len(lhs_list), alpha=alpha, n_exp=n_exp, top_k=TOP_K)
    row = lambda i: (i, 0)
    fixed = lambda i: (0, 0)
    in_specs = [pl.BlockSpec((tm, l.shape[1]), row) for l in lhs_list] + [
        pl.BlockSpec(w.shape, fixed), pl.BlockSpec((tm, d), row),
        pl.BlockSpec((1, d), fixed), pl.BlockSpec((1, d), fixed),
        pl.BlockSpec((n_exp, d), fixed), pl.BlockSpec((n_exp, 1), fixed), pl.BlockSpec((tm, tm), fixed)]
    tok = lambda i: (0, i)
    return pl.pallas_call(
        kern,
        grid=(n // tm,),
        in_specs=in_specs,
        out_specs=[pl.BlockSpec((tm, d), row), pl.BlockSpec((TOP_K, tm), tok), pl.BlockSpec((TOP_K, tm), tok),
                   pl.BlockSpec((TOP_K, tm), tok), pl.BlockSpec((n_exp, LANES), fixed)],
        out_shape=[jax.ShapeDtypeStruct((n, d), F32), jax.ShapeDtypeStruct((TOP_K, n), I32),
                   jax.ShapeDtypeStruct((TOP_K, n), F32), jax.ShapeDtypeStruct((TOP_K, n), I32),
                   jax.ShapeDtypeStruct((n_exp, LANES), F32)],
        scratch_shapes=[pltpu.VMEM((n_exp, LANES), F32)],
        compiler_params=_cparams(("arbitrary",)),
        name="proj_ln_router",
    )(*lhs_list, w, x, ln_g, ln_b, wr_t, br, tri)


def _dispatch_kernel(dest_ref, x_ref, xs_in_ref, xs_ref, xp_ref, sem, *, top_k):
    del xs_in_ref
    tm, d = x_ref.shape
    x = x_ref[...]
    lo = lax.bitcast_convert_type(x[:, :d // 2].astype(BF16).astype(F32), U32)
    hi = lax.bitcast_convert_type(x[:, d // 2:].astype(BF16).astype(F32), U32)
    xp_ref[...] = jnp.bitwise_or(lax.shift_right_logical(lo, jnp.uint32(16)),
                                 jnp.bitwise_and(hi, jnp.uint32(0xFFFF0000))).reshape(xp_ref.shape)

    for kk in range(top_k):
        def issue(g, carry, kk=kk):
            for r in range(SUBLANES):
                dst = xs_ref.at[pl.ds(dest_ref[kk * tm + g * SUBLANES + r], 1)]
                pltpu.make_async_copy(xp_ref.at[g, pl.ds(r, 1)], dst, sem).start(priority=r % DMA_PRIORITIES)
            return carry

        lax.fori_loop(0, tm // SUBLANES, issue, 0)
    for _ in range(top_k):
        pltpu.make_async_copy(xs_ref.at[pl.ds(0, tm)], xs_ref.at[pl.ds(0, tm)], sem).wait()


def _dispatch(dest_flat, x1, xs_zero, tm):
    n, d = x1.shape
    kern = functools.partial(_dispatch_kernel, top_k=TOP_K)
    return pl.pallas_call(
        kern,
        grid=(n // tm,),
        in_specs=[pl.BlockSpec((TOP_K * tm,), lambda i: (i,), memory_space=pltpu.SMEM),
                  pl.BlockSpec((tm, d), lambda i: (i, 0)),
                  pl.BlockSpec(memory_space=pl.ANY)],
        out_specs=pl.BlockSpec(memory_space=pl.ANY),
        out_shape=jax.ShapeDtypeStruct(xs_zero.shape, U32),
        scratch_shapes=[pltpu.VMEM((tm // SUBLANES, SUBLANES, d // 2), U32), pltpu.SemaphoreType.DMA(())],
        input_output_aliases={2: 0},
        compiler_params=_cparams(("arbitrary",)),
        name="moe_dispatch",
    )(dest_flat, x1, xs_zero)


def _expert_kernel(be_ref, nv_ref, xs_ref, wg_ref, wu_ref, bg_ref, bu_ref, wd_ref, bd_ref, y_ref,
                   xb_ref):
    del be_ref
    i, j = pl.program_id(0), pl.program_id(1)
    half = xs_ref.shape[1]

    @pl.when(j == 0)
    def _():
        u = xs_ref[...]
        lo = lax.bitcast_convert_type(lax.shift_left(u, jnp.uint32(16)), F32)
        hi = lax.bitcast_convert_type(jnp.bitwise_and(u, jnp.uint32(0xFFFF0000)), F32)
        xb_ref[:, :half] = lo.astype(BF16)
        xb_ref[:, half:] = hi.astype(BF16)
        y_ref[...] = jnp.broadcast_to(bd_ref[0], y_ref.shape)

    @pl.when(i < nv_ref[0])
    def _():
        xb = xb_ref[...]
        hg = jnp.dot(xb, wg_ref[0], preferred_element_type=F32) + bg_ref[0]
        hu = jnp.dot(xb, wu_ref[0], preferred_element_type=F32) + bu_ref[0]
        gate = jnp.minimum(hg, SWIGLU_LIMIT)
        up = jnp.clip(hu, -SWIGLU_LIMIT, SWIGLU_LIMIT)
        glu = gate / (1.0 + jnp.exp(-SWIGLU_ALPHA * gate))
        y_ref[...] += jnp.dot(((up + 1.0) * glu).astype(BF16), wd_ref[0], preferred_element_type=F32)


def _expert_ffn(block_e, n_valid, xs, wg, wu, bg, bu, wd, bd, tm, tn):
    p, half = xs.shape
    d = 2 * half
    d_ff = wg.shape[2]
    nj = d_ff // tn

    def blk(i, nv):
        return jnp.minimum(i, nv[0] - 1)

    def jj(i, j, nv):
        return jnp.where(i < nv[0], j, nj - 1)

    grid_spec = pltpu.PrefetchScalarGridSpec(
        num_scalar_prefetch=2,
        grid=(p // tm, nj),
        in_specs=[pl.BlockSpec((tm, half), lambda i, j, be, nv: (blk(i, nv), 0)),
                  pl.BlockSpec((1, d, tn), lambda i, j, be, nv: (be[blk(i, nv)], 0, jj(i, j, nv))),
                  pl.BlockSpec((1, d, tn), lambda i, j, be, nv: (be[blk(i, nv)], 0, jj(i, j, nv))),
                  pl.BlockSpec((1, 1, tn), lambda i, j, be, nv: (be[blk(i, nv)], 0, jj(i, j, nv))),
                  pl.BlockSpec((1, 1, tn), lambda i, j, be, nv: (be[blk(i, nv)], 0, jj(i, j, nv))),
                  pl.BlockSpec((1, tn, d), lambda i, j, be, nv: (be[blk(i, nv)], jj(i, j, nv), 0)),
                  pl.BlockSpec((1, 1, d), lambda i, j, be, nv: (be[blk(i, nv)], 0, 0))],
        out_specs=pl.BlockSpec((tm, d), lambda i, j, be, nv: (i, 0)),
        scratch_shapes=[pltpu.VMEM((tm, d), BF16)])
    return pl.pallas_call(
        _expert_kernel,
        grid_spec=grid_spec,
        out_shape=jax.ShapeDtypeStruct((p, d), F32),
        compiler_params=_cparams(("arbitrary", "arbitrary")),
        name="moe_experts",
    )(block_e, n_valid, xs, wg, wu, bg, bu, wd, bd)


def _combine_kernel(dest_ref, x_ref, gates_ref, g_ref, b_ref, y_ref, y3_ref, o_ref, buf_ref, sem, *,
                    top_k, alpha):
    tm, d = x_ref.shape

    for kk in range(top_k):
        def issue(g, carry, kk=kk):
            for r in range(SUBLANES):
                src = y_ref.at[pl.ds(dest_ref[kk * tm + g * SUBLANES + r], 1)]
                pltpu.make_async_copy(src, buf_ref.at[kk, g, pl.ds(r, 1)], sem).start(priority=r % DMA_PRIORITIES)
            return carry

        lax.fori_loop(0, tm // SUBLANES, issue, 0)
    for kk in range(top_k):
        pltpu.make_async_copy(y3_ref.at[pl.ds(0, tm // SUBLANES)], buf_ref.at[kk], sem).wait()
    ff = gates_ref[:, 0:1] * buf_ref[0].reshape(tm, d)
    for kk in range(1, top_k):
        ff = ff + gates_ref[:, kk:kk + 1] * buf_ref[kk].reshape(tm, d)
    o_ref[...] = _layer_norm(alpha * x_ref[...] + ff, g_ref[...], b_ref[...])


def _combine(dest_flat, x1, gates_t, ln_g, ln_b, y, alpha, tm):
    n, d = x1.shape
    kern = functools.partial(_combine_kernel, top_k=TOP_K, alpha=alpha)
    return pl.pallas_call(
        kern,
        grid=(n // tm,),
        in_specs=[pl.BlockSpec((TOP_K * tm,), lambda i: (i,), memory_space=pltpu.SMEM),
                  pl.BlockSpec((tm, d), lambda i: (i, 0)),
                  pl.BlockSpec((tm, TOP_K), lambda i: (i, 0)),
                  pl.BlockSpec((1, d), lambda i: (0, 0)),
                  pl.BlockSpec((1, d), lambda i: (0, 0)),
                  pl.BlockSpec(memory_space=pl.ANY),
                  pl.BlockSpec(memory_space=pl.ANY)],
        out_specs=pl.BlockSpec((tm, d), lambda i: (i, 0)),
        out_shape=jax.ShapeDtypeStruct((n, d), F32),
        scratch_shapes=[pltpu.VMEM((TOP_K, tm // SUBLANES, SUBLANES, d), F32), pltpu.SemaphoreType.DMA(())],
        compiler_params=_cparams(("arbitrary",)),
        name="moe_combine",
    )(dest_flat, x1, gates_t, ln_g, ln_b, y, y.reshape(y.shape[0] // SUBLANES, SUBLANES, d))


MOE_ROW_TILE = 1024
MOE_FF_TILE = 512
MOE_TOKEN_TILE = 256
ROUTER_TILE = 256


def _moe_layer(x1, idx, gates, rank, cnt, layer, ew, ln_g, ln_b, alpha, xs_buf):
    n, d = x1.shape
    n_exp = cnt.shape[0]
    tm_e = _tile(n * TOP_K, MOE_ROW_TILE)
    tok = _tile(n, MOE_TOKEN_TILE)
    counts = cnt[:, 0].astype(I32)
    padded = ((counts + tm_e - 1) // tm_e) * tm_e
    pend = jnp.cumsum(padded)
    pstart = pend - padded
    dest = rank
    for e in range(n_exp):
        dest = dest + jnp.where(idx == e, pstart[e], 0)
    n_blocks = (n * TOP_K) // tm_e + n_exp
    starts = jnp.arange(n_blocks, dtype=I32) * tm_e
    block_e = jnp.minimum(jnp.sum((starts[:, None] >= pend[None, :]).astype(I32), axis=1), n_exp - 1) + layer * n_exp
    n_valid = (pend[-1:] // tm_e).astype(I32)
    dest_flat = dest.reshape(TOP_K, n // tok, tok).transpose(1, 0, 2).reshape(-1)
    if xs_buf is None:
        xs_buf = jnp.zeros((n_blocks * tm_e, d // 2), U32)
    xs = _dispatch(dest_flat, x1, xs_buf, tok)
    y = _expert_ffn(block_e, n_valid, xs, *ew, tm_e, _tile(ew[0].shape[2], MOE_FF_TILE))
    return _combine(dest_flat, x1, gates.T, ln_g, ln_b, y, alpha, tok), xs


def kernel(x_prompt, x_sample, e_w_in, e_w_pool, e_pool_scale, e_w_gate2, e_b_gate2, e_gla_norm_g, e_w_out,
           o_w_qkv, o_lam_q1, o_lam_k1, o_lam_q2, o_lam_k2, o_subln_g, o_w_out,
           ln1_g, ln1_b, ln2_g, ln2_b,
           moe_w_router, moe_b_router, moe_w_gate_up, moe_b_gate_up, moe_w_down, moe_b_down):
    depth = ln1_g.shape[0]
    alpha = (2.0 * depth) ** 0.25
    nb_p, seq, d = x_prompt.shape
    assert x_sample.shape[1:] == (seq, d)
    x = jnp.concatenate([x_prompt, x_sample], axis=0)
    bsz = x.shape[0]
    n = bsz * seq
    x = x.reshape(n, d)

    n_exp = moe_w_router.shape[2]
    d_ff = moe_w_down.shape[2]
    w_gate, w_up = _deinterleave_bf16(moe_w_gate_up.reshape(depth * n_exp, d, 2 * d_ff))
    ew = (w_gate, w_up,
          moe_b_gate_up[..., 0::2].reshape(depth * n_exp, 1, d_ff),
          moe_b_gate_up[..., 1::2].reshape(depth * n_exp, 1, d_ff),
          _cast_bf16(moe_w_down.reshape(depth * n_exp, d_ff, d)),
          moe_b_down.reshape(depth * n_exp, 1, d))
    wr_t = jnp.swapaxes(moe_w_router, 1, 2).astype(BF16)
    tm_r = _tile(n, ROUTER_TILE)
    tri = (jnp.arange(tm_r)[:, None] < jnp.arange(tm_r)[None, :]).astype(BF16)

    pool_width = e_w_pool.shape[1] * e_w_pool.shape[2]
    gla_width = d - pool_width
    keyw = gla_width // 2
    main_w = pool_width + 2 * keyw + 2 * gla_width
    hd = d // (2 * DIFF_HEADS)

    xs_buf = None
    for l in range(depth):
        i = l // 2
        if l % 2 == 0:
            w_in = e_w_in[i]
            h_main = _matmul(x, w_in[:, :main_w].astype(BF16), F32, "in_proj").reshape(bsz, seq, main_w)
            w_gl = jnp.pad(w_in[:, main_w:], ((0, 0), (0, LANES - 2 * GLA_GATE_RANK))).astype(BF16)
            h_gate = _matmul(x, w_gl, F32, "in_proj_gates").reshape(bsz, seq, LANES)
            a_out = _pool_mixer(h_main, e_w_pool[i].astype(BF16), e_pool_scale[i][None, :], pool_width)
            wg2 = e_w_gate2[i].astype(BF16)
            zpad = jnp.zeros((LANES - 2 * GLA_GATE_RANK, keyw), BF16)
            zrank = jnp.zeros((GLA_GATE_RANK, keyw), BF16)
            wgf = jnp.concatenate([wg2[0], zrank, zpad], axis=0)
            wgb = jnp.concatenate([zrank, wg2[1], zpad], axis=0)
            b_out = _gla_mixer(h_main, h_gate, wgf, wgb, e_b_gate2[i][0][None, :], e_b_gate2[i][1][None, :],
                               e_gla_norm_g[i][None, :], pool_width, GLA_HEADS)
            lhs = [a_out.reshape(n, pool_width), b_out.reshape(n, gla_width)]
            w_out = e_w_out[i].astype(BF16)
        else:
            lam_init = 0.8 - 0.6 * math.exp(-0.3 * l)
            cos, sin = _rope_tables(seq, hd)
            qkv = _qkv_proj(x, o_w_qkv[i].astype(BF16), cos, sin, seq, d, hd).reshape(bsz, seq, 3 * d)
            attn = _diff_attention(qkv, o_lam_q1[i][None, :], o_lam_k1[i][None, :], o_lam_q2[i][None, :],
                                   o_lam_k2[i][None, :], o_subln_g[i][None, :], lam_init, d, DIFF_HEADS)
            lhs = [attn.reshape(n, d)]
            w_out = o_w_out[i].astype(BF16)
        x1, idx, gates, rank, cnt = _proj_ln_router(
            lhs, w_out, x, ln1_g[l][None, :], ln1_b[l][None, :], wr_t[l], moe_b_router[l][:, None], tri,
            alpha, tm_r)
        x, xs_buf = _moe_layer(x1, idx, gates, rank, cnt, l, ew, ln2_g[l][None, :], ln2_b[l][None, :], alpha,
                               xs_buf)

    x = x.reshape(bsz, seq, d)
    return (x[:nb_p], x[nb_p:])
```

```python
import functools
import math

import jax
import jax.numpy as jnp
from jax import lax
from jax.experimental import pallas as pl
from jax.experimental.pallas import tpu as pltpu

F32 = jnp.float32
BF16 = jnp.bfloat16
I32 = jnp.int32
U32 = jnp.uint32

POOL_WINDOWS = (2, 4, 8, 16)
GLA_HEADS = 4
GLA_GATE_RANK = 16
GLA_GATE_NORM = 16.0
GLA_CHUNK = 64
DIFF_HEADS = 8
ROPE_THETA = 10000.0
TOP_K = 4
SWIGLU_LIMIT = 7.0
SWIGLU_ALPHA = 1.702
LN_EPS = 1e-5

V7X_VMEM_BYTES = 64 * 2 ** 20
VMEM_LIMIT = V7X_VMEM_BYTES - 8 * 2 ** 20
LANES = 128
SUBLANES = 8
SUBLANE_SHIFT = 3
DMA_PRIORITIES = 2


def _tile(n, pref):
    t = min(n, pref)
    while n % t:
        t //= 2
    return t


def _cparams(sem):
    return pltpu.CompilerParams(dimension_semantics=sem, vmem_limit_bytes=VMEM_LIMIT)


def _nt_dot(a, b):
    return lax.dot_general(a, b, (((1,), (1,)), ((), ())), preferred_element_type=F32)


def _layer_norm(y, g, b):
    mu = jnp.mean(y, axis=-1, keepdims=True)
    yc = y - mu
    var = jnp.mean(yc * yc, axis=-1, keepdims=True)
    return yc * lax.rsqrt(var + LN_EPS) * g + b


def _part_bounds(parts, tm):
    bounds, lo = [], 0
    for p in parts:
        assert p.shape[0] % tm == 0, (p.shape, tm)
        bounds.append((lo, lo + p.shape[0] // tm))
        lo = bounds[-1][1]
    return tuple(bounds)


def _part_specs(parts, tm, bounds):
    def spec(p, lo, hi):
        return pl.BlockSpec((tm, p.shape[1]), lambda i, *_: (jnp.clip(i - lo, 0, hi - lo - 1), 0))
    return [spec(p, lo, hi) for p, (lo, hi) in zip(parts, bounds)]


def _part_value(i, refs, bounds):
    val = refs[-1][...]
    for ref, (_, hi) in zip(reversed(refs[:-1]), reversed(bounds[:-1])):
        val = jnp.where(i < hi, ref[...], val)
    return val


def _mm_kernel(*refs, bounds):
    x_refs = refs[:len(bounds)]
    w_ref, o_ref, xb_ref = refs[len(bounds):]

    @pl.when(pl.program_id(1) == 0)
    def _():
        xb_ref[...] = _part_value(pl.program_id(0), x_refs, bounds).astype(BF16)

    o_ref[...] = jnp.dot(xb_ref[...], w_ref[...], preferred_element_type=F32).astype(o_ref.dtype)


def _matmul(x_parts, w, out_dtype, name, tm_pref=1024, tn_pref=1024):
    m = sum(p.shape[0] for p in x_parts)
    k, n = w.shape
    tm = _tile(math.gcd(*[p.shape[0] for p in x_parts]), tm_pref)
    tn = _tile(n, tn_pref)
    bounds = _part_bounds(x_parts, tm)
    return pl.pallas_call(
        functools.partial(_mm_kernel, bounds=bounds),
        grid=(m // tm, n // tn),
        in_specs=_part_specs(x_parts, tm, bounds) + [pl.BlockSpec((k, tn), lambda i, j: (0, j))],
        out_specs=pl.BlockSpec((tm, tn), lambda i, j: (i, j)),
        out_shape=jax.ShapeDtypeStruct((m, n), out_dtype),
        scratch_shapes=[pltpu.VMEM((tm, k), BF16)],
        compiler_params=_cparams(("parallel", "arbitrary")),
        name=name,
    )(*x_parts, w)


MXU_TILE = 256


def _cast_kernel(x_ref, o_ref):
    o_ref[...] = x_ref[...].astype(o_ref.dtype)


def _cast_bf16(x):
    a, r, c = x.shape
    tr = _tile(r, 512)
    return pl.pallas_call(
        _cast_kernel,
        grid=(a, r // tr),
        in_specs=[pl.BlockSpec((1, tr, c), lambda i, j: (i, j, 0))],
        out_specs=pl.BlockSpec((1, tr, c), lambda i, j: (i, j, 0)),
        out_shape=jax.ShapeDtypeStruct(x.shape, BF16),
        compiler_params=_cparams(("parallel", "parallel")),
        name="cast_bf16",
    )(x)


def _deinterleave_kernel(w_ref, p_ref, g_ref, u_ref):
    w = w_ref[0].astype(BF16)
    hm = MXU_TILE // 2
    for c in range(w.shape[1] // MXU_TILE):
        r = jnp.dot(w[:, c * MXU_TILE:(c + 1) * MXU_TILE], p_ref[...], preferred_element_type=F32).astype(BF16)
        g_ref[0, :, c * hm:(c + 1) * hm] = r[:, :hm]
        u_ref[0, :, c * hm:(c + 1) * hm] = r[:, hm:]


def _deinterleave_bf16(w):
    a, r, c2 = w.shape
    tr, tc2 = _tile(r, 1024), _tile(c2, 1024)
    assert tc2 % MXU_TILE == 0
    rows = jnp.arange(MXU_TILE)
    perm = (jnp.arange(MXU_TILE)[None, :] == (rows // 2 + (rows % 2) * (MXU_TILE // 2))[:, None]).astype(BF16)
    out = jax.ShapeDtypeStruct((a, r, c2 // 2), BF16)
    return pl.pallas_call(
        _deinterleave_kernel,
        grid=(a, r // tr, c2 // tc2),
        in_specs=[pl.BlockSpec((1, tr, tc2), lambda i, j, k: (i, j, k)),
                  pl.BlockSpec((MXU_TILE, MXU_TILE), lambda i, j, k: (0, 0))],
        out_specs=[pl.BlockSpec((1, tr, tc2 // 2), lambda i, j, k: (i, j, k))] * 2,
        out_shape=[out, out],
        compiler_params=_cparams(("parallel", "parallel", "parallel")),
        name="deinterleave_bf16",
    )(w, perm)


def _rope_table_kernel(cos_ref, sin_ref, *, half, theta):
    rows, hd = cos_ref.shape
    pos = (lax.broadcasted_iota(I32, (rows, hd), 0) + pl.program_id(0) * rows).astype(F32)
    lane = lax.broadcasted_iota(I32, (rows, hd), 1)
    fi = jnp.where(lane >= half, lane - half, lane).astype(F32)
    inv = jnp.exp(fi * (-math.log(theta) / half))
    ang = pos * inv
    cos_ref[...] = jnp.cos(ang)
    s = jnp.sin(ang)
    sin_ref[...] = jnp.where(lane < half, -s, s)


def _rope_tables(seq, hd):
    rows = _tile(seq, 256)
    return pl.pallas_call(
        functools.partial(_rope_table_kernel, half=hd // 2, theta=ROPE_THETA),
        grid=(seq // rows,),
        out_specs=[pl.BlockSpec((rows, hd), lambda i: (i, 0))] * 2,
        out_shape=[jax.ShapeDtypeStruct((seq, hd), F32)] * 2,
        compiler_params=_cparams(("parallel",)),
        name="rope_tables",
    )()


def _qkv_kernel(x_ref, w_ref, cos_ref, sin_ref, o_ref, xb_ref, *, n_q_tiles, n_rope_tiles, hd, q_scale):
    j = pl.program_id(1)

    @pl.when(j == 0)
    def _():
        xb_ref[...] = x_ref[...].astype(BF16)

    acc = jnp.dot(xb_ref[...], w_ref[...], preferred_element_type=F32)
    tn = acc.shape[1]

    rot = jnp.where(j < n_rope_tiles, jnp.where(j < n_q_tiles, q_scale, 1.0), 0.0)
    cos = cos_ref[...] * rot + jnp.where(j < n_rope_tiles, 0.0, 1.0)
    sin = sin_ref[...] * rot
    for h in range(tn // hd):
        t = acc[:, h * hd:(h + 1) * hd]
        r = t * cos + pltpu.roll(t, hd // 2, axis=1) * sin
        o_ref[:, h * hd:(h + 1) * hd] = r.astype(o_ref.dtype)


def _qkv_proj(x, w, cos, sin, seq, d_model, hd):
    m, k = x.shape
    n = w.shape[1]
    tm, tn = _tile(seq, 1024), _tile(d_model, 512)
    spb = seq // tm
    kern = functools.partial(_qkv_kernel, n_q_tiles=d_model // tn, n_rope_tiles=2 * d_model // tn,
                             hd=hd, q_scale=hd ** -0.5)
    return pl.pallas_call(
        kern,
        grid=(m // tm, n // tn),
        in_specs=[pl.BlockSpec((tm, k), lambda i, j: (i, 0)),
                  pl.BlockSpec((k, tn), lambda i, j: (0, j)),
                  pl.BlockSpec((tm, hd), lambda i, j: (i % spb, 0)),
                  pl.BlockSpec((tm, hd), lambda i, j: (i % spb, 0))],
        out_specs=pl.BlockSpec((tm, tn), lambda i, j: (i, j)),
        out_shape=jax.ShapeDtypeStruct((m, n), BF16),
        scratch_shapes=[pltpu.VMEM((tm, k), BF16)],
        compiler_params=_cparams(("parallel", "arbitrary")),
        name="qkv_rope",
    )(x, w, cos, sin)


POOL_HALO = 8


def _pool_kernel(u_ref, w_ref, sc_ref, o_ref, pad_ref, *, windows, gw, seq, rc):
    assert max(windows) // 2 <= POOL_HALO
    pad_ref[0:POOL_HALO, :] = jnp.zeros((POOL_HALO, gw), F32)
    pad_ref[POOL_HALO + seq:POOL_HALO + seq + POOL_HALO, :] = jnp.zeros((POOL_HALO, gw), F32)
    for g, w in enumerate(windows):
        hw = w // 2
        pad_ref[POOL_HALO:POOL_HALO + seq, :] = u_ref[0, :, g * gw:(g + 1) * gw]
        for c in range(seq // rc):
            base = POOL_HALO + c * rc
            acc = pad_ref[base - hw:base - hw + rc, :]
            for off in range(-hw + 1, hw):
                acc = acc + pad_ref[base + off:base + off + rc, :]
            pos = lax.broadcasted_iota(I32, (rc, gw), 0) + c * rc
            cnt = (jnp.minimum(pos + hw, seq) - jnp.maximum(pos - hw, 0)).astype(F32)
            pooled = acc / cnt - pad_ref[base:base + rc, :]
            mixed = jnp.dot(pooled.astype(BF16), w_ref[g], preferred_element_type=F32)
            mixed = mixed * sc_ref[:, g * gw:(g + 1) * gw]
            o_ref[0, c * rc:(c + 1) * rc, g * gw:(g + 1) * gw] = mixed.astype(o_ref.dtype)


def _pool_mixer(h_main, w_pool, pool_scale, pool_width):
    b, seq, _ = h_main.shape
    g = w_pool.shape[0]
    gw = pool_width // g
    rc = _tile(seq, 256)
    kern = functools.partial(_pool_kernel, windows=POOL_WINDOWS, gw=gw, seq=seq, rc=rc)
    return pl.pallas_call(
        kern,
        grid=(b,),
        in_specs=[pl.BlockSpec((1, seq, pool_width), lambda i: (i, 0, 0)),
                  pl.BlockSpec((g, gw, gw), lambda i: (0, 0, 0)),
                  pl.BlockSpec((1, pool_width), lambda i: (0, 0))],
        out_specs=pl.BlockSpec((1, seq, pool_width), lambda i: (i, 0, 0)),
        out_shape=jax.ShapeDtypeStruct((b, seq, pool_width), BF16),
        scratch_shapes=[pltpu.VMEM((seq + 2 * POOL_HALO, gw), F32)],
        compiler_params=_cparams(("parallel",)),
        name="pool_mixer",
    )(h_main, w_pool, pool_scale)


def _log_sigmoid(x):
    return jnp.minimum(x, 0.0) - jnp.log(1.0 + jnp.exp(-jnp.abs(x)))


def _gla_kernel(q_ref, k_ref, v_ref, r_ref, gl_ref, wgf_ref, wgb_ref, bgf_ref, bgb_ref, gn_ref, o_ref,
                qm_f, km_f, kd_f, qx_f, kx_f, qm_b, km_b, kd_b, qx_b, kx_b,
                vt_ref, dec_f, dec_b, of_ref, ob_ref, sf_ref, sb_ref, *, seq, dk, dv, chunk, rb):
    pair = 2 * chunk
    n_pairs = seq // pair
    q_scale = dk ** -0.5

    def prep(blk, carry):
        r0 = pl.multiple_of(blk * rb, rb)
        rows = pl.ds(r0, rb)
        q = q_ref[0, rows, :] * q_scale
        k = k_ref[0, rows, :]
        gl = gl_ref[0, rows, :].astype(BF16)
        row = lax.broadcasted_iota(I32, (rb, dk), 0)
        rc = jnp.bitwise_and(row, chunk - 1)
        second = jnp.bitwise_and(row, chunk) != 0

        def scans(pre):
            g = _log_sigmoid(pre) / GLA_GATE_NORM
            pfx = g
            sfx = g
            sh = 1
            while sh < chunk:
                pfx = pfx + jnp.where(rc >= sh, pltpu.roll(pfx, sh, axis=0), 0.0)
                sfx = sfx + jnp.where(rc + sh < chunk, pltpu.roll(sfx, rb - sh, axis=0), 0.0)
                sh *= 2
            tot = pfx + sfx - g
            return pfx, sfx, tot

        def emit(bcum, tot, q_extra_rows, qm, km, kd, qx, kx, dec):
            tot_prev = pltpu.roll(tot, chunk, axis=0)
            tot_next = pltpu.roll(tot, rb - chunk, axis=0)
            other = jnp.where(second, tot_prev, tot_next)
            qm[rows, :] = (q * jnp.exp(bcum)).astype(BF16)
            km[rows, :] = (k * jnp.exp(-bcum)).astype(BF16)
            kd[rows, :] = (k * jnp.exp(tot - bcum)).astype(BF16)
            qx[rows, :] = (q * jnp.exp(bcum + jnp.where(q_extra_rows, other, 0.0))).astype(BF16)
            kx[rows, :] = (k * jnp.exp(tot - bcum + jnp.where(q_extra_rows, 0.0, other))).astype(BF16)
            d = jnp.exp(tot + other)
            for p in range(rb // pair):
                dec[blk * (rb // pair) + p] = d[p * pair:p * pair + 1, :]

        pre_f = jnp.dot(gl, wgf_ref[...], preferred_element_type=F32) + bgf_ref[...]
        pfx, _, tot = scans(pre_f)
        emit(pfx, tot, second, qm_f, km_f, kd_f, qx_f, kx_f, dec_f)
        pre_b = jnp.dot(gl, wgb_ref[...], preferred_element_type=F32) + bgb_ref[...]
        _, sfx, tot = scans(pre_b)
        emit(sfx, tot, jnp.logical_not(second), qm_b, km_b, kd_b, qx_b, kx_b, dec_b)
        return carry

    lax.fori_loop(0, seq // rb, prep, 0)

    for p in range(n_pairs):
        vt_ref[p] = v_ref[0, p * pair:(p + 1) * pair, :].T.astype(BF16)

    ri = lax.broadcasted_iota(I32, (pair, pair), 0)
    ci = lax.broadcasted_iota(I32, (pair, pair), 1)
    same = (ri >= chunk) == (ci >= chunk)
    m1_f = jnp.logical_and(same, ci <= ri)
    m2_f = jnp.logical_and(ri >= chunk, ci < chunk)
    m1_b = jnp.logical_and(same, ci >= ri)
    m2_b = jnp.logical_and(ri < chunk, ci >= chunk)

    sf_ref[...] = jnp.zeros((dv, dk), F32)
    sb_ref[...] = jnp.zeros((dv, dk), F32)

    def step(p, qm, km, kd, qx, kx, dec, s_ref, out, m1, m2):
        rows = pl.ds(pl.multiple_of(p * pair, pair), pair)
        a1 = _nt_dot(qm[rows, :], km[rows, :])
        a2 = _nt_dot(qm[rows, :], kd[rows, :])
        att = jnp.where(m1, a1, jnp.where(m2, a2, 0.0)).astype(BF16)
        vb = v_ref[0, rows, :].astype(BF16)
        st = s_ref[...]
        o = jnp.dot(att, vb, preferred_element_type=F32) + _nt_dot(qx[rows, :], st.astype(BF16))
        out[rows, :] = o
        s_ref[...] = st * dec[p] + jnp.dot(vt_ref[p], kx[rows, :], preferred_element_type=F32)

    def scan(i, carry):
        step(i, qm_f, km_f, kd_f, qx_f, kx_f, dec_f, sf_ref, of_ref, m1_f, m2_f)
        step(n_pairs - 1 - i, qm_b, km_b, kd_b, qx_b, kx_b, dec_b, sb_ref, ob_ref, m1_b, m2_b)
        return carry

    lax.fori_loop(0, n_pairs, scan, 0)

    def finish(blk, carry):
        rows = pl.ds(pl.multiple_of(blk * rb, rb), rb)
        o = of_ref[rows, :] + ob_ref[rows, :]
        mu = jnp.mean(o, axis=-1, keepdims=True)
        oc = o - mu
        var = jnp.mean(oc * oc, axis=-1, keepdims=True)
        on = oc * lax.rsqrt(var + LN_EPS) * gn_ref[...]
        r = r_ref[0, rows, :]
        silu = r / (1.0 + jnp.exp(-r))
        o_ref[0, rows, :] = (silu * on).astype(o_ref.dtype)
        return carry

    lax.fori_loop(0, seq // rb, finish, 0)


def _gla_mixer(h_main, h_gate, wgf, wgb, bgf, bgb, gla_norm_g, pool_width, heads):
    b, seq, _ = h_main.shape
    gla_width = gla_norm_g.shape[1]
    dv = gla_width // heads
    dk = dv // 2
    keyw = heads * dk
    chunk = GLA_CHUNK
    pair = 2 * chunk
    assert seq % pair == 0 and chunk & (chunk - 1) == 0
    rb = _tile(seq, 256)
    assert rb % pair == 0
    q0, k0 = pool_width // dk, (pool_width + keyw) // dk
    v0, r0 = (pool_width + 2 * keyw) // dv, (pool_width + 2 * keyw + gla_width) // dv
    assert pool_width % dk == 0 and (pool_width + 2 * keyw) % dv == 0
    gk = h_gate.shape[2]
    kern = functools.partial(_gla_kernel, seq=seq, dk=dk, dv=dv, chunk=chunk, rb=rb)
    row_dk = pltpu.VMEM((seq, dk), BF16)
    return pl.pallas_call(
        kern,
        grid=(b, heads),
        in_specs=[pl.BlockSpec((1, seq, dk), lambda i, h: (i, 0, q0 + h)),
                  pl.BlockSpec((1, seq, dk), lambda i, h: (i, 0, k0 + h)),
                  pl.BlockSpec((1, seq, dv), lambda i, h: (i, 0, v0 + h)),
                  pl.BlockSpec((1, seq, dv), lambda i, h: (i, 0, r0 + h)),
                  pl.BlockSpec((1, seq, gk), lambda i, h: (i, 0, 0)),
                  pl.BlockSpec((gk, dk), lambda i, h: (0, h)),
                  pl.BlockSpec((gk, dk), lambda i, h: (0, h)),
                  pl.BlockSpec((1, dk), lambda i, h: (0, h)),
                  pl.BlockSpec((1, dk), lambda i, h: (0, h)),
                  pl.BlockSpec((1, dv), lambda i, h: (0, h))],
        out_specs=pl.BlockSpec((1, seq, dv), lambda i, h: (i, 0, h)),
        out_shape=jax.ShapeDtypeStruct((b, seq, gla_width), BF16),
        scratch_shapes=[row_dk] * 10 + [
            pltpu.VMEM((seq // pair, dv, pair), BF16),
            pltpu.VMEM((seq // pair, 1, dk), F32),
            pltpu.VMEM((seq // pair, 1, dk), F32),
            pltpu.VMEM((seq, dv), F32),
            pltpu.VMEM((seq, dv), F32),
            pltpu.VMEM((dv, dk), F32),
            pltpu.VMEM((dv, dk), F32)],
        compiler_params=_cparams(("parallel", "arbitrary")),
        name="gla_mixer",
    )(h_main, h_main, h_main, h_main, h_gate, wgf, wgb, bgf, bgb, gla_norm_g)


ATTN_Q_SUBBLOCK = 128


def _diff_attn_kernel(q1_ref, q2_ref, k1_ref, k2_ref, v_ref, lq1_ref, lk1_ref, lq2_ref, lk2_ref, sg_ref,
                      o_ref, *, lam_init, qb):
    lam = (jnp.exp(jnp.sum(lq1_ref[...] * lk1_ref[...], axis=-1, keepdims=True))
           - jnp.exp(jnp.sum(lq2_ref[...] * lk2_ref[...], axis=-1, keepdims=True)) + lam_init)
    v = v_ref[0]
    k1 = k1_ref[0]
    k2 = k2_ref[0]

    def branch(q, k):
        s = _nt_dot(q, k)
        m = jnp.max(s, axis=-1, keepdims=True)
        p = jnp.exp(s - m)
        l = jnp.sum(p, axis=-1, keepdims=True)
        return jnp.dot(p.astype(BF16), v, preferred_element_type=F32) / l

    for c in range(q1_ref.shape[1] // qb):
        rows = slice(c * qb, (c + 1) * qb)
        o = branch(q1_ref[0, rows, :], k1) - lam * branch(q2_ref[0, rows, :], k2)
        rms = lax.rsqrt(jnp.mean(o * o, axis=-1, keepdims=True) + LN_EPS)
        o_ref[0, rows, :] = (o * rms * sg_ref[...] * (1.0 - lam_init)).astype(o_ref.dtype)


def _diff_attention(qkv, lam_q1, lam_k1, lam_q2, lam_k2, subln_g, lam_init, d_model, heads):
    b, seq, _ = qkv.shape
    hd = d_model // (2 * heads)
    tq = _tile(seq, 512)
    k0 = d_model // hd
    v0 = 2 * d_model // (2 * hd)
    kern = functools.partial(_diff_attn_kernel, lam_init=lam_init, qb=_tile(tq, ATTN_Q_SUBBLOCK))
    vec = pl.BlockSpec((1, hd), lambda i, h, qi: (0, 0))
    return pl.pallas_call(
        kern,
        grid=(b, heads, seq // tq),
        in_specs=[pl.BlockSpec((1, tq, hd), lambda i, h, qi: (i, qi, 2 * h)),
                  pl.BlockSpec((1, tq, hd), lambda i, h, qi: (i, qi, 2 * h + 1)),
                  pl.BlockSpec((1, seq, hd), lambda i, h, qi: (i, 0, k0 + 2 * h)),
                  pl.BlockSpec((1, seq, hd), lambda i, h, qi: (i, 0, k0 + 2 * h + 1)),
                  pl.BlockSpec((1, seq, 2 * hd), lambda i, h, qi: (i, 0, v0 + h)),
                  vec, vec, vec, vec,
                  pl.BlockSpec((1, 2 * hd), lambda i, h, qi: (0, 0))],
        out_specs=pl.BlockSpec((1, tq, 2 * hd), lambda i, h, qi: (i, qi, h)),
        out_shape=jax.ShapeDtypeStruct((b, seq, d_model), BF16),
        compiler_params=_cparams(("parallel", "parallel", "arbitrary")),
        name="diff_attention",
    )(qkv, qkv, qkv, qkv, qkv, lam_q1, lam_k1, lam_q2, lam_k2, subln_g)


def _proj_ln_router_kernel(*refs, n_lhs, x_bounds, alpha, n_exp, top_k):
    lhs = refs[:n_lhs]
    w_ref = refs[n_lhs]
    x_refs = refs[n_lhs + 1:n_lhs + 1 + len(x_bounds)]
    rest = refs[n_lhs + 1 + len(x_bounds):]
    g_ref, b_ref, wr_ref, br_ref, tri_ref = rest[:5]
    x1_ref, idx_ref, gate_ref, rank_ref, cnt_ref = rest[5:10]
    run_ref = rest[10]

    @pl.when(pl.program_id(0) == 0)
    def _():
        run_ref[...] = jnp.zeros(run_ref.shape, F32)

    off = 0
    acc = None
    for l in lhs:
        kl = l.shape[1]
        part = jnp.dot(l[...], w_ref[off:off + kl, :], preferred_element_type=F32)
        acc = part if acc is None else acc + part
        off += kl
    x_res = _part_value(pl.program_id(0), x_refs, x_bounds)
    x1 = _layer_norm(alpha * x_res + acc, g_ref[...], b_ref[...])
    x1_ref[...] = x1

    tm = x1.shape[0]
    logits = _nt_dot(wr_ref[...], x1.astype(BF16)) + br_ref[...]
    eidx = lax.broadcasted_iota(I32, (n_exp, tm), 0)
    vals, idxs, hots = [], [], []
    for _ in range(top_k):
        m = jnp.max(logits, axis=0, keepdims=True)
        idx = jnp.min(jnp.where(logits == m, eidx, n_exp - 1), axis=0, keepdims=True)
        hot = eidx == idx
        vals.append(m)
        idxs.append(idx)
        hots.append(hot)
        logits = jnp.where(hot, -jnp.inf, logits)
    exps = [jnp.exp(v - vals[0]) for v in vals]
    den = exps[0]
    for e in exps[1:]:
        den = den + e
    gate_ref[...] = jnp.concatenate([e / den for e in exps], axis=0)
    idx_ref[...] = jnp.concatenate(idxs, axis=0)

    hot_all = jnp.concatenate([h.astype(BF16) for h in hots], axis=0)
    before = jnp.dot(hot_all, tri_ref[...], preferred_element_type=F32)
    run = run_ref[:, 0:1]
    ranks = []
    for kk in range(top_k):
        hf = hots[kk].astype(F32)
        ranks.append(jnp.sum(hf * (run + before[kk * n_exp:(kk + 1) * n_exp, :]), axis=0, keepdims=True))
        run = run + jnp.sum(hf, axis=1, keepdims=True)
    rank_ref[...] = jnp.concatenate(ranks, axis=0).astype(I32)
    run_ref[...] = jnp.broadcast_to(run, run_ref.shape)
    cnt_ref[...] = run_ref[...]


def _proj_ln_router(lhs_list, w, x_parts, ln_g, ln_b, wr_t, br, tri, alpha, tm):
    n, d = sum(p.shape[0] for p in x_parts), x_parts[0].shape[1]
    n_exp = wr_t.shape[0]
    x_bounds = _part_bounds(x_parts, tm)
    kern = functools.partial(_proj_ln_router_kernel, n_lhs=len(lhs_list), x_bounds=x_bounds, alpha=alpha,
                             n_exp=n_exp, top_k=TOP_K)
    row = lambda i: (i, 0)
    fixed = lambda i: (0, 0)
    in_specs = [pl.BlockSpec((tm, l.shape[1]), row) for l in lhs_list] + [
        pl.BlockSpec(w.shape, fixed)] + _part_specs(x_parts, tm, x_bounds) + [
        pl.BlockSpec((1, d), fixed), pl.BlockSpec((1, d), fixed),
        pl.BlockSpec((n_exp, d), fixed), pl.BlockSpec((n_exp, 1), fixed), pl.BlockSpec((tm, tm), fixed)]
    tok = lambda i: (0, i)
    return pl.pallas_call(
        kern,
        grid=(n // tm,),
        in_specs=in_specs,
        out_specs=[pl.BlockSpec((tm, d), row), pl.BlockSpec((TOP_K, tm), tok), pl.BlockSpec((TOP_K, tm), tok),
                   pl.BlockSpec((TOP_K, tm), tok), pl.BlockSpec((n_exp, LANES), fixed)],
        out_shape=[jax.ShapeDtypeStruct((n, d), F32), jax.ShapeDtypeStruct((TOP_K, n), I32),
                   jax.ShapeDtypeStruct((TOP_K, n), F32), jax.ShapeDtypeStruct((TOP_K, n), I32),
                   jax.ShapeDtypeStruct((n_exp, LANES), F32)],
        scratch_shapes=[pltpu.VMEM((n_exp, LANES), F32)],
        compiler_params=_cparams(("arbitrary",)),
        name="proj_ln_router",
    )(*lhs_list, w, *x_parts, ln_g, ln_b, wr_t, br, tri)


def _dispatch_kernel(dest_ref, x_ref, xs_in_ref, xs_ref, xp_ref, sem, *, top_k):
    del xs_in_ref
    tm, d = x_ref.shape
    x = x_ref[...]
    lo = lax.bitcast_convert_type(x[:, :d // 2].astype(BF16).astype(F32), U32)
    hi = lax.bitcast_convert_type(x[:, d // 2:].astype(BF16).astype(F32), U32)
    xp_ref[...] = jnp.bitwise_or(lax.shift_right_logical(lo, jnp.uint32(16)),
                                 jnp.bitwise_and(hi, jnp.uint32(0xFFFF0000))).reshape(xp_ref.shape)

    for kk in range(top_k):
        def issue(g, carry, kk=kk):
            for r in range(SUBLANES):
                dst = xs_ref.at[pl.ds(dest_ref[kk * tm + g * SUBLANES + r], 1)]
                pltpu.make_async_copy(xp_ref.at[g, pl.ds(r, 1)], dst, sem).start(priority=r % DMA_PRIORITIES)
            return carry

        lax.fori_loop(0, tm // SUBLANES, issue, 0)
    for _ in range(top_k):
        pltpu.make_async_copy(xs_ref.at[pl.ds(0, tm)], xs_ref.at[pl.ds(0, tm)], sem).wait()


def _dispatch(dest_flat, x1, xs_zero, tm):
    n, d = x1.shape
    kern = functools.partial(_dispatch_kernel, top_k=TOP_K)
    return pl.pallas_call(
        kern,
        grid=(n // tm,),
        in_specs=[pl.BlockSpec((TOP_K * tm,), lambda i: (i,), memory_space=pltpu.SMEM),
                  pl.BlockSpec((tm, d), lambda i: (i, 0)),
                  pl.BlockSpec(memory_space=pl.ANY)],
        out_specs=pl.BlockSpec(memory_space=pl.ANY),
        out_shape=jax.ShapeDtypeStruct(xs_zero.shape, U32),
        scratch_shapes=[pltpu.VMEM((tm // SUBLANES, SUBLANES, d // 2), U32), pltpu.SemaphoreType.DMA(())],
        input_output_aliases={2: 0},
        compiler_params=_cparams(("arbitrary",)),
        name="moe_dispatch",
    )(dest_flat, x1, xs_zero)


def _expert_kernel(be_ref, nv_ref, xs_ref, wg_ref, wu_ref, bg_ref, bu_ref, wd_ref, bd_ref, y_ref,
                   xb_ref, act_ref, *, n_ff_tiles):
    del be_ref
    i, j = pl.program_id(0), pl.program_id(1)
    half = xs_ref.shape[1]
    valid = i < nv_ref[0]

    @pl.when(jnp.logical_and(valid, j == 0))
    def _():
        u = xs_ref[...]
        lo = lax.bitcast_convert_type(lax.shift_left(u, jnp.uint32(16)), F32)
        hi = lax.bitcast_convert_type(jnp.bitwise_and(u, jnp.uint32(0xFFFF0000)), F32)
        xb_ref[:, :half] = lo.astype(BF16)
        xb_ref[:, half:] = hi.astype(BF16)

    @pl.when(jnp.logical_and(valid, j < n_ff_tiles))
    def _():
        xb = xb_ref[...]
        hg = jnp.dot(xb, wg_ref[0], preferred_element_type=F32) + bg_ref[0]
        hu = jnp.dot(xb, wu_ref[0], preferred_element_type=F32) + bu_ref[0]
        gate = jnp.minimum(hg, SWIGLU_LIMIT)
        up = jnp.clip(hu, -SWIGLU_LIMIT, SWIGLU_LIMIT)
        glu = gate / (1.0 + jnp.exp(-SWIGLU_ALPHA * gate))
        act_ref[j] = ((up + 1.0) * glu).astype(BF16)

    @pl.when(jnp.logical_and(valid, j >= n_ff_tiles))
    def _():
        act = jnp.concatenate([act_ref[c] for c in range(n_ff_tiles)], axis=1)
        y_ref[...] = jnp.dot(act, wd_ref[0], preferred_element_type=F32) + bd_ref[0]

    @pl.when(jnp.logical_and(jnp.logical_not(valid), j >= n_ff_tiles))
    def _():
        y_ref[...] = jnp.broadcast_to(bd_ref[0], y_ref.shape)


def _expert_ffn(block_e, n_valid, xs, wg, wu, bg, bu, wd, bd, tm, tn, tn_out):
    p, half = xs.shape
    d = 2 * half
    d_ff = wg.shape[2]
    nf, nd = d_ff // tn, d // tn_out

    def blk(i, nv):
        return jnp.minimum(i, nv[0] - 1)

    def ff_tile(i, j, nv):
        return jnp.where(i < nv[0], jnp.minimum(j, nf - 1), nf - 1)

    def out_tile(i, j, nv):
        return jnp.where(i < nv[0], jnp.maximum(j - nf, 0), nd - 1)

    grid_spec = pltpu.PrefetchScalarGridSpec(
        num_scalar_prefetch=2,
        grid=(p // tm, nf + nd),
        in_specs=[pl.BlockSpec((tm, half), lambda i, j, be, nv: (blk(i, nv), 0)),
                  pl.BlockSpec((1, d, tn), lambda i, j, be, nv: (be[blk(i, nv)], 0, ff_tile(i, j, nv))),
                  pl.BlockSpec((1, d, tn), lambda i, j, be, nv: (be[blk(i, nv)], 0, ff_tile(i, j, nv))),
                  pl.BlockSpec((1, 1, tn), lambda i, j, be, nv: (be[blk(i, nv)], 0, ff_tile(i, j, nv))),
                  pl.BlockSpec((1, 1, tn), lambda i, j, be, nv: (be[blk(i, nv)], 0, ff_tile(i, j, nv))),
                  pl.BlockSpec((1, d_ff, tn_out), lambda i, j, be, nv: (be[blk(i, nv)], 0, out_tile(i, j, nv))),
                  pl.BlockSpec((1, 1, tn_out), lambda i, j, be, nv: (be[blk(i, nv)], 0, out_tile(i, j, nv)))],
        out_specs=pl.BlockSpec((tm, tn_out), lambda i, j, be, nv: (i, jnp.maximum(j - nf, 0))),
        scratch_shapes=[pltpu.VMEM((tm, d), BF16), pltpu.VMEM((nf, tm, tn), BF16)])
    return pl.pallas_call(
        functools.partial(_expert_kernel, n_ff_tiles=nf),
        grid_spec=grid_spec,
        out_shape=jax.ShapeDtypeStruct((p, d), F32),
        compiler_params=_cparams(("arbitrary", "arbitrary")),
        name="moe_experts",
    )(block_e, n_valid, xs, wg, wu, bg, bu, wd, bd)


def _combine_kernel(dest_ref, x_ref, gates_ref, g_ref, b_ref, y_ref, y3_ref, *refs, top_k, alpha, out_bounds):
    o_refs = refs[:len(out_bounds)]
    buf_ref, sem = refs[len(out_bounds):]
    tm, d = x_ref.shape

    for kk in range(top_k):
        def issue(g, carry, kk=kk):
            for r in range(SUBLANES):
                src = y_ref.at[pl.ds(dest_ref[kk * tm + g * SUBLANES + r], 1)]
                pltpu.make_async_copy(src, buf_ref.at[kk, g, pl.ds(r, 1)], sem).start(priority=r % DMA_PRIORITIES)
            return carry

        lax.fori_loop(0, tm // SUBLANES, issue, 0)
    for kk in range(top_k):
        pltpu.make_async_copy(y3_ref.at[pl.ds(0, tm // SUBLANES)], buf_ref.at[kk], sem).wait()
    ff = gates_ref[:, 0:1] * buf_ref[0].reshape(tm, d)
    for kk in range(1, top_k):
        ff = ff + gates_ref[:, kk:kk + 1] * buf_ref[kk].reshape(tm, d)
    res = _layer_norm(alpha * x_ref[...] + ff, g_ref[...], b_ref[...])
    if len(o_refs) == 1:
        o_refs[0][...] = res
    else:
        i = pl.program_id(0)
        for o_ref, (lo, hi) in zip(o_refs, out_bounds):
            @pl.when(jnp.logical_and(i >= lo, i < hi))
            def _(o_ref=o_ref):
                o_ref[...] = res


def _combine(dest_flat, x1, gates_t, ln_g, ln_b, y, alpha, tm, out_rows):
    n, d = x1.shape
    assert sum(out_rows) == n
    outs = [jax.ShapeDtypeStruct((r, d), F32) for r in out_rows]
    out_bounds = _part_bounds(outs, tm)
    kern = functools.partial(_combine_kernel, top_k=TOP_K, alpha=alpha, out_bounds=out_bounds)
    return pl.pallas_call(
        kern,
        grid=(n // tm,),
        in_specs=[pl.BlockSpec((TOP_K * tm,), lambda i: (i,), memory_space=pltpu.SMEM),
                  pl.BlockSpec((tm, d), lambda i: (i, 0)),
                  pl.BlockSpec((tm, TOP_K), lambda i: (i, 0)),
                  pl.BlockSpec((1, d), lambda i: (0, 0)),
                  pl.BlockSpec((1, d), lambda i: (0, 0)),
                  pl.BlockSpec(memory_space=pl.ANY),
                  pl.BlockSpec(memory_space=pl.ANY)],
        out_specs=_part_specs(outs, tm, out_bounds),
        out_shape=outs,
        scratch_shapes=[pltpu.VMEM((TOP_K, tm // SUBLANES, SUBLANES, d), F32), pltpu.SemaphoreType.DMA(())],
        compiler_params=_cparams(("arbitrary",)),
        name="moe_combine",
    )(dest_flat, x1, gates_t, ln_g, ln_b, y, y.reshape(y.shape[0] // SUBLANES, SUBLANES, d))


MOE_ROW_TILE = 1024
MOE_FF_TILE = 512
MOE_OUT_TILE = 512
MOE_TOKEN_TILE = 256
ROUTER_TILE = 256


def _moe_layer(x1, idx, gates, rank, cnt, layer, ew, ln_g, ln_b, alpha, xs_buf, out_rows):
    n, d = x1.shape
    n_exp = cnt.shape[0]
    tm_e = _tile(n * TOP_K, MOE_ROW_TILE)
    tok = _tile(n, MOE_TOKEN_TILE)
    counts = cnt[:, 0].astype(I32)
    padded = ((counts + tm_e - 1) // tm_e) * tm_e
    pend = jnp.cumsum(padded)
    pstart = pend - padded
    dest = rank
    for e in range(n_exp):
        dest = dest + jnp.where(idx == e, pstart[e], 0)
    n_blocks = (n * TOP_K) // tm_e + n_exp
    starts = jnp.arange(n_blocks, dtype=I32) * tm_e
    block_e = jnp.minimum(jnp.sum((starts[:, None] >= pend[None, :]).astype(I32), axis=1), n_exp - 1) + layer * n_exp
    n_valid = (pend[-1:] // tm_e).astype(I32)
    dest_flat = dest.reshape(TOP_K, n // tok, tok).transpose(1, 0, 2).reshape(-1)
    if xs_buf is None:
        xs_buf = jnp.zeros((n_blocks * tm_e, d // 2), U32)
    xs = _dispatch(dest_flat, x1, xs_buf, tok)
    y = _expert_ffn(block_e, n_valid, xs, *ew, tm_e, _tile(ew[0].shape[2], MOE_FF_TILE), _tile(d, MOE_OUT_TILE))
    return _combine(dest_flat, x1, gates.T, ln_g, ln_b, y, alpha, tok, out_rows), xs


def kernel(x_prompt, x_sample, e_w_in, e_w_pool, e_pool_scale, e_w_gate2, e_b_gate2, e_gla_norm_g, e_w_out,
           o_w_qkv, o_lam_q1, o_lam_k1, o_lam_q2, o_lam_k2, o_subln_g, o_w_out,
           ln1_g, ln1_b, ln2_g, ln2_b,
           moe_w_router, moe_b_router, moe_w_gate_up, moe_b_gate_up, moe_w_down, moe_b_down):
    depth = ln1_g.shape[0]
    alpha = (2.0 * depth) ** 0.25
    nb_p, seq, d = x_prompt.shape
    assert x_sample.shape[1:] == (seq, d)
    bsz = nb_p + x_sample.shape[0]
    n = bsz * seq
    x = [x_prompt.reshape(nb_p * seq, d), x_sample.reshape(n - nb_p * seq, d)]
    final_rows = tuple(p.shape[0] for p in x)

    n_exp = moe_w_router.shape[2]
    d_ff = moe_w_down.shape[2]
    w_gate, w_up = _deinterleave_bf16(moe_w_gate_up.reshape(depth * n_exp, d, 2 * d_ff))
    ew = (w_gate, w_up,
          moe_b_gate_up[..., 0::2].reshape(depth * n_exp, 1, d_ff),
          moe_b_gate_up[..., 1::2].reshape(depth * n_exp, 1, d_ff),
          _cast_bf16(moe_w_down.reshape(depth * n_exp, d_ff, d)),
          moe_b_down.reshape(depth * n_exp, 1, d))
    wr_t = jnp.swapaxes(moe_w_router, 1, 2).astype(BF16)
    tm_r = _tile(n, ROUTER_TILE)
    tri = (jnp.arange(tm_r)[:, None] < jnp.arange(tm_r)[None, :]).astype(BF16)

    pool_width = e_w_pool.shape[1] * e_w_pool.shape[2]
    gla_width = d - pool_width
    keyw = gla_width // 2
    main_w = pool_width + 2 * keyw + 2 * gla_width
    hd = d // (2 * DIFF_HEADS)

    xs_buf = None
    for l in range(depth):
        i = l // 2
        if l % 2 == 0:
            w_in = e_w_in[i]
            h_main = _matmul(x, w_in[:, :main_w].astype(BF16), F32, "in_proj").reshape(bsz, seq, main_w)
            w_gl = jnp.pad(w_in[:, main_w:], ((0, 0), (0, LANES - 2 * GLA_GATE_RANK))).astype(BF16)
            h_gate = _matmul(x, w_gl, F32, "in_proj_gates").reshape(bsz, seq, LANES)
            a_out = _pool_mixer(h_main, e_w_pool[i].astype(BF16), e_pool_scale[i][None, :], pool_width)
            wg2 = e_w_gate2[i].astype(BF16)
            zpad = jnp.zeros((LANES - 2 * GLA_GATE_RANK, keyw), BF16)
            zrank = jnp.zeros((GLA_GATE_RANK, keyw), BF16)
            wgf = jnp.concatenate([wg2[0], zrank, zpad], axis=0)
            wgb = jnp.concatenate([zrank, wg2[1], zpad], axis=0)
            b_out = _gla_mixer(h_main, h_gate, wgf, wgb, e_b_gate2[i][0][None, :], e_b_gate2[i][1][None, :],
                               e_gla_norm_g[i][None, :], pool_width, GLA_HEADS)
            lhs = [a_out.reshape(n, pool_width), b_out.reshape(n, gla_width)]
            w_out = e_w_out[i].astype(BF16)
        else:
            lam_init = 0.8 - 0.6 * math.exp(-0.3 * l)
            cos, sin = _rope_tables(seq, hd)
            x_all = x[0] if len(x) == 1 else jnp.concatenate(x, axis=0)
            qkv = _qkv_proj(x_all, o_w_qkv[i].astype(BF16), cos, sin, seq, d, hd).reshape(bsz, seq, 3 * d)
            attn = _diff_attention(qkv, o_lam_q1[i][None, :], o_lam_k1[i][None, :], o_lam_q2[i][None, :],
                                   o_lam_k2[i][None, :], o_subln_g[i][None, :], lam_init, d, DIFF_HEADS)
            lhs = [attn.reshape(n, d)]
            w_out = o_w_out[i].astype(BF16)
        x1, idx, gates, rank, cnt = _proj_ln_router(
            lhs, w_out, x, ln1_g[l][None, :], ln1_b[l][None, :], wr_t[l], moe_b_router[l][:, None], tri,
            alpha, tm_r)
        out_rows = final_rows if l == depth - 1 else (n,)
        x, xs_buf = _moe_layer(x1, idx, gates, rank, cnt, l, ew, ln2_g[l][None, :], ln2_b[l][None, :], alpha,
                               xs_buf, out_rows)

    return (x[0].reshape(x_prompt.shape), x[1].reshape(x_sample.shape))
```

```python
import functools
import math

import jax
import jax.numpy as jnp
from jax import lax
from jax.experimental import pallas as pl
from jax.experimental.pallas import tpu as pltpu

F32 = jnp.float32
BF16 = jnp.bfloat16
I32 = jnp.int32
U32 = jnp.uint32

POOL_WINDOWS = (2, 4, 8, 16)
GLA_HEADS = 4
GLA_GATE_RANK = 16
GLA_GATE_NORM = 16.0
GLA_CHUNK = 64
DIFF_HEADS = 8
ROPE_THETA = 10000.0
TOP_K = 4
SWIGLU_LIMIT = 7.0
SWIGLU_ALPHA = 1.702
LN_EPS = 1e-5

V7X_VMEM_BYTES = 64 * 2 ** 20
VMEM_LIMIT = V7X_VMEM_BYTES - 8 * 2 ** 20
LANES = 128
SUBLANES = 8
SUBLANE_SHIFT = 3
DMA_PRIORITIES = 2


def _tile(n, pref):
    t = min(n, pref)
    while n % t:
        t //= 2
    return t


def _cparams(sem):
    return pltpu.CompilerParams(dimension_semantics=sem, vmem_limit_bytes=VMEM_LIMIT)


def _nt_dot(a, b):
    return lax.dot_general(a, b, (((1,), (1,)), ((), ())), preferred_element_type=F32)


def _layer_norm(y, g, b):
    mu = jnp.mean(y, axis=-1, keepdims=True)
    yc = y - mu
    var = jnp.mean(yc * yc, axis=-1, keepdims=True)
    return yc * lax.rsqrt(var + LN_EPS) * g + b


def _part_bounds(parts, tm):
    bounds, lo = [], 0
    for p in parts:
        assert p.shape[0] % tm == 0, (p.shape, tm)
        bounds.append((lo, lo + p.shape[0] // tm))
        lo = bounds[-1][1]
    return tuple(bounds)


def _part_specs(parts, tm, bounds):
    def spec(p, lo, hi):
        return pl.BlockSpec((tm, p.shape[1]), lambda i, *_: (jnp.clip(i - lo, 0, hi - lo - 1), 0))
    return [spec(p, lo, hi) for p, (lo, hi) in zip(parts, bounds)]


def _part_value(i, refs, bounds):
    val = refs[-1][...]
    for ref, (_, hi) in zip(reversed(refs[:-1]), reversed(bounds[:-1])):
        val = jnp.where(i < hi, ref[...], val)
    return val


def _mm_kernel(*refs, bounds):
    x_refs = refs[:len(bounds)]
    w_ref, o_ref, xb_ref = refs[len(bounds):]

    @pl.when(pl.program_id(1) == 0)
    def _():
        xb_ref[...] = _part_value(pl.program_id(0), x_refs, bounds).astype(BF16)

    o_ref[...] = jnp.dot(xb_ref[...], w_ref[...], preferred_element_type=F32).astype(o_ref.dtype)


def _matmul(x_parts, w, out_dtype, name, tm_pref=1024, tn_pref=1024):
    m = sum(p.shape[0] for p in x_parts)
    k, n = w.shape
    tm = _tile(math.gcd(*[p.shape[0] for p in x_parts]), tm_pref)
    tn = _tile(n, tn_pref)
    bounds = _part_bounds(x_parts, tm)
    return pl.pallas_call(
        functools.partial(_mm_kernel, bounds=bounds),
        grid=(m // tm, n // tn),
        in_specs=_part_specs(x_parts, tm, bounds) + [pl.BlockSpec((k, tn), lambda i, j: (0, j))],
        out_specs=pl.BlockSpec((tm, tn), lambda i, j: (i, j)),
        out_shape=jax.ShapeDtypeStruct((m, n), out_dtype),
        scratch_shapes=[pltpu.VMEM((tm, k), BF16)],
        compiler_params=_cparams(("parallel", "arbitrary")),
        name=name,
    )(*x_parts, w)


MXU_TILE = 256


def _cast_kernel(x_ref, o_ref):
    o_ref[...] = x_ref[...].astype(o_ref.dtype)


def _cast_bf16(x):
    a, r, c = x.shape
    tr = _tile(r, 512)
    return pl.pallas_call(
        _cast_kernel,
        grid=(a, r // tr),
        in_specs=[pl.BlockSpec((1, tr, c), lambda i, j: (i, j, 0))],
        out_specs=pl.BlockSpec((1, tr, c), lambda i, j: (i, j, 0)),
        out_shape=jax.ShapeDtypeStruct(x.shape, BF16),
        compiler_params=_cparams(("parallel", "parallel")),
        name="cast_bf16",
    )(x)


def _deinterleave_kernel(w_ref, p_ref, g_ref, u_ref):
    w = w_ref[0].astype(BF16)
    hm = MXU_TILE // 2
    for c in range(w.shape[1] // MXU_TILE):
        r = jnp.dot(w[:, c * MXU_TILE:(c + 1) * MXU_TILE], p_ref[...], preferred_element_type=F32).astype(BF16)
        g_ref[0, :, c * hm:(c + 1) * hm] = r[:, :hm]
        u_ref[0, :, c * hm:(c + 1) * hm] = r[:, hm:]


def _deinterleave_bf16(w):
    a, r, c2 = w.shape
    tr, tc2 = _tile(r, 1024), _tile(c2, 1024)
    assert tc2 % MXU_TILE == 0
    rows = jnp.arange(MXU_TILE)
    perm = (jnp.arange(MXU_TILE)[None, :] == (rows // 2 + (rows % 2) * (MXU_TILE // 2))[:, None]).astype(BF16)
    out = jax.ShapeDtypeStruct((a, r, c2 // 2), BF16)
    return pl.pallas_call(
        _deinterleave_kernel,
        grid=(a, r // tr, c2 // tc2),
        in_specs=[pl.BlockSpec((1, tr, tc2), lambda i, j, k: (i, j, k)),
                  pl.BlockSpec((MXU_TILE, MXU_TILE), lambda i, j, k: (0, 0))],
        out_specs=[pl.BlockSpec((1, tr, tc2 // 2), lambda i, j, k: (i, j, k))] * 2,
        out_shape=[out, out],
        compiler_params=_cparams(("parallel", "parallel", "parallel")),
        name="deinterleave_bf16",
    )(w, perm)


def _rope_table_kernel(cos_ref, sin_ref, *, half, theta):
    rows, hd = cos_ref.shape
    pos = (lax.broadcasted_iota(I32, (rows, hd), 0) + pl.program_id(0) * rows).astype(F32)
    lane = lax.broadcasted_iota(I32, (rows, hd), 1)
    fi = jnp.where(lane >= half, lane - half, lane).astype(F32)
    inv = jnp.exp(fi * (-math.log(theta) / half))
    ang = pos * inv
    cos_ref[...] = jnp.cos(ang)
    s = jnp.sin(ang)
    sin_ref[...] = jnp.where(lane < half, -s, s)


def _rope_tables(seq, hd):
    rows = _tile(seq, 256)
    return pl.pallas_call(
        functools.partial(_rope_table_kernel, half=hd // 2, theta=ROPE_THETA),
        grid=(seq // rows,),
        out_specs=[pl.BlockSpec((rows, hd), lambda i: (i, 0))] * 2,
        out_shape=[jax.ShapeDtypeStruct((seq, hd), F32)] * 2,
        compiler_params=_cparams(("parallel",)),
        name="rope_tables",
    )()


def _qkv_kernel(x_ref, w_ref, cos_ref, sin_ref, o_ref, xb_ref, *, n_q_tiles, n_rope_tiles, hd, q_scale):
    j = pl.program_id(1)

    @pl.when(j == 0)
    def _():
        xb_ref[...] = x_ref[...].astype(BF16)

    acc = jnp.dot(xb_ref[...], w_ref[...], preferred_element_type=F32)
    tn = acc.shape[1]

    rot = jnp.where(j < n_rope_tiles, jnp.where(j < n_q_tiles, q_scale, 1.0), 0.0)
    cos = cos_ref[...] * rot + jnp.where(j < n_rope_tiles, 0.0, 1.0)
    sin = sin_ref[...] * rot
    for h in range(tn // hd):
        t = acc[:, h * hd:(h + 1) * hd]
        r = t * cos + pltpu.roll(t, hd // 2, axis=1) * sin
        o_ref[:, h * hd:(h + 1) * hd] = r.astype(o_ref.dtype)


def _qkv_proj(x, w, cos, sin, seq, d_model, hd):
    m, k = x.shape
    n = w.shape[1]
    tm, tn = _tile(seq, 1024), _tile(d_model, 512)
    spb = seq // tm
    kern = functools.partial(_qkv_kernel, n_q_tiles=d_model // tn, n_rope_tiles=2 * d_model // tn,
                             hd=hd, q_scale=hd ** -0.5)
    return pl.pallas_call(
        kern,
        grid=(m // tm, n // tn),
        in_specs=[pl.BlockSpec((tm, k), lambda i, j: (i, 0)),
                  pl.BlockSpec((k, tn), lambda i, j: (0, j)),
                  pl.BlockSpec((tm, hd), lambda i, j: (i % spb, 0)),
                  pl.BlockSpec((tm, hd), lambda i, j: (i % spb, 0))],
        out_specs=pl.BlockSpec((tm, tn), lambda i, j: (i, j)),
        out_shape=jax.ShapeDtypeStruct((m, n), BF16),
        scratch_shapes=[pltpu.VMEM((tm, k), BF16)],
        compiler_params=_cparams(("parallel", "arbitrary")),
        name="qkv_rope",
    )(x, w, cos, sin)


POOL_HALO = 8


def _pool_kernel(u_ref, w_ref, sc_ref, o_ref, pad_ref, *, windows, gw, seq, rc):
    assert max(windows) // 2 <= POOL_HALO
    pad_ref[0:POOL_HALO, :] = jnp.zeros((POOL_HALO, gw), F32)
    pad_ref[POOL_HALO + seq:POOL_HALO + seq + POOL_HALO, :] = jnp.zeros((POOL_HALO, gw), F32)
    for g, w in enumerate(windows):
        hw = w // 2
        pad_ref[POOL_HALO:POOL_HALO + seq, :] = u_ref[0, :, g * gw:(g + 1) * gw]
        for c in range(seq // rc):
            base = POOL_HALO + c * rc
            acc = pad_ref[base - hw:base - hw + rc, :]
            for off in range(-hw + 1, hw):
                acc = acc + pad_ref[base + off:base + off + rc, :]
            pos = lax.broadcasted_iota(I32, (rc, gw), 0) + c * rc
            cnt = (jnp.minimum(pos + hw, seq) - jnp.maximum(pos - hw, 0)).astype(F32)
            pooled = acc / cnt - pad_ref[base:base + rc, :]
            mixed = jnp.dot(pooled.astype(BF16), w_ref[g], preferred_element_type=F32)
            mixed = mixed * sc_ref[:, g * gw:(g + 1) * gw]
            o_ref[0, c * rc:(c + 1) * rc, g * gw:(g + 1) * gw] = mixed.astype(o_ref.dtype)


def _pool_mixer(h_main, w_pool, pool_scale, pool_width):
    b, seq, _ = h_main.shape
    g = w_pool.shape[0]
    gw = pool_width // g
    rc = _tile(seq, 256)
    kern = functools.partial(_pool_kernel, windows=POOL_WINDOWS, gw=gw, seq=seq, rc=rc)
    return pl.pallas_call(
        kern,
        grid=(b,),
        in_specs=[pl.BlockSpec((1, seq, pool_width), lambda i: (i, 0, 0)),
                  pl.BlockSpec((g, gw, gw), lambda i: (0, 0, 0)),
                  pl.BlockSpec((1, pool_width), lambda i: (0, 0))],
        out_specs=pl.BlockSpec((1, seq, pool_width), lambda i: (i, 0, 0)),
        out_shape=jax.ShapeDtypeStruct((b, seq, pool_width), BF16),
        scratch_shapes=[pltpu.VMEM((seq + 2 * POOL_HALO, gw), F32)],
        compiler_params=_cparams(("parallel",)),
        name="pool_mixer",
    )(h_main, w_pool, pool_scale)


def _log_sigmoid(x):
    return jnp.minimum(x, 0.0) - jnp.log(1.0 + jnp.exp(-jnp.abs(x)))


def _gla_kernel(q_ref, k_ref, v_ref, r_ref, gl_ref, wgf_ref, wgb_ref, bgf_ref, bgb_ref, gn_ref, o_ref,
                qm_f, km_f, kd_f, qx_f, kx_f, qm_b, km_b, kd_b, qx_b, kx_b,
                vt_ref, dec_f, dec_b, of_ref, ob_ref, sf_ref, sb_ref, *, seq, dk, dv, chunk, rb):
    pair = 2 * chunk
    n_pairs = seq // pair
    q_scale = dk ** -0.5

    def prep(blk, carry):
        r0 = pl.multiple_of(blk * rb, rb)
        rows = pl.ds(r0, rb)
        q = q_ref[0, rows, :] * q_scale
        k = k_ref[0, rows, :]
        gl = gl_ref[0, rows, :].astype(BF16)
        row = lax.broadcasted_iota(I32, (rb, dk), 0)
        rc = jnp.bitwise_and(row, chunk - 1)
        second = jnp.bitwise_and(row, chunk) != 0

        def scans(pre):
            g = _log_sigmoid(pre) / GLA_GATE_NORM
            pfx = g
            sfx = g
            sh = 1
            while sh < chunk:
                pfx = pfx + jnp.where(rc >= sh, pltpu.roll(pfx, sh, axis=0), 0.0)
                sfx = sfx + jnp.where(rc + sh < chunk, pltpu.roll(sfx, rb - sh, axis=0), 0.0)
                sh *= 2
            tot = pfx + sfx - g
            return pfx, sfx, tot

        def emit(bcum, tot, q_extra_rows, qm, km, kd, qx, kx, dec):
            tot_prev = pltpu.roll(tot, chunk, axis=0)
            tot_next = pltpu.roll(tot, rb - chunk, axis=0)
            other = jnp.where(second, tot_prev, tot_next)
            qm[rows, :] = (q * jnp.exp(bcum)).astype(BF16)
            km[rows, :] = (k * jnp.exp(-bcum)).astype(BF16)
            kd[rows, :] = (k * jnp.exp(tot - bcum)).astype(BF16)
            qx[rows, :] = (q * jnp.exp(bcum + jnp.where(q_extra_rows, other, 0.0))).astype(BF16)
            kx[rows, :] = (k * jnp.exp(tot - bcum + jnp.where(q_extra_rows, 0.0, other))).astype(BF16)
            d = jnp.exp(tot + other)
            for p in range(rb // pair):
                dec[blk * (rb // pair) + p] = d[p * pair:p * pair + 1, :]

        pre_f = jnp.dot(gl, wgf_ref[...], preferred_element_type=F32) + bgf_ref[...]
        pfx, _, tot = scans(pre_f)
        emit(pfx, tot, second, qm_f, km_f, kd_f, qx_f, kx_f, dec_f)
        pre_b = jnp.dot(gl, wgb_ref[...], preferred_element_type=F32) + bgb_ref[...]
        _, sfx, tot = scans(pre_b)
        emit(sfx, tot, jnp.logical_not(second), qm_b, km_b, kd_b, qx_b, kx_b, dec_b)
        return carry

    lax.fori_loop(0, seq // rb, prep, 0)

    for p in range(n_pairs):
        vt_ref[p] = v_ref[0, p * pair:(p + 1) * pair, :].T.astype(BF16)

    ri = lax.broadcasted_iota(I32, (pair, pair), 0)
    ci = lax.broadcasted_iota(I32, (pair, pair), 1)
    same = (ri >= chunk) == (ci >= chunk)
    m1_f = jnp.logical_and(same, ci <= ri)
    m2_f = jnp.logical_and(ri >= chunk, ci < chunk)
    m1_b = jnp.logical_and(same, ci >= ri)
    m2_b = jnp.logical_and(ri < chunk, ci >= chunk)

    sf_ref[...] = jnp.zeros((dv, dk), F32)
    sb_ref[...] = jnp.zeros((dv, dk), F32)

    def step(p, qm, km, kd, qx, kx, dec, s_ref, out, m1, m2):
        rows = pl.ds(pl.multiple_of(p * pair, pair), pair)
        a1 = _nt_dot(qm[rows, :], km[rows, :])
        a2 = _nt_dot(qm[rows, :], kd[rows, :])
        att = jnp.where(m1, a1, jnp.where(m2, a2, 0.0)).astype(BF16)
        vb = v_ref[0, rows, :].astype(BF16)
        st = s_ref[...]
        o = jnp.dot(att, vb, preferred_element_type=F32) + _nt_dot(qx[rows, :], st.astype(BF16))
        out[rows, :] = o
        s_ref[...] = st * dec[p] + jnp.dot(vt_ref[p], kx[rows, :], preferred_element_type=F32)

    def scan(i, carry):
        step(i, qm_f, km_f, kd_f, qx_f, kx_f, dec_f, sf_ref, of_ref, m1_f, m2_f)
        step(n_pairs - 1 - i, qm_b, km_b, kd_b, qx_b, kx_b, dec_b, sb_ref, ob_ref, m1_b, m2_b)
        return carry

    lax.fori_loop(0, n_pairs, scan, 0)

    def finish(blk, carry):
        rows = pl.ds(pl.multiple_of(blk * rb, rb), rb)
        o = of_ref[rows, :] + ob_ref[rows, :]
        mu = jnp.mean(o, axis=-1, keepdims=True)
        oc = o - mu
        var = jnp.mean(oc * oc, axis=-1, keepdims=True)
        on = oc * lax.rsqrt(var + LN_EPS) * gn_ref[...]
        r = r_ref[0, rows, :]
        silu = r / (1.0 + jnp.exp(-r))
        o_ref[0, rows, :] = (silu * on).astype(o_ref.dtype)
        return carry

    lax.fori_loop(0, seq // rb, finish, 0)


def _gla_mixer(h_main, h_gate, wgf, wgb, bgf, bgb, gla_norm_g, pool_width, heads):
    b, seq, _ = h_main.shape
    gla_width = gla_norm_g.shape[1]
    dv = gla_width // heads
    dk = dv // 2
    keyw = heads * dk
    chunk = GLA_CHUNK
    pair = 2 * chunk
    assert seq % pair == 0 and chunk & (chunk - 1) == 0
    rb = _tile(seq, 256)
    assert rb % pair == 0
    q0, k0 = pool_width // dk, (pool_width + keyw) // dk
    v0, r0 = (pool_width + 2 * keyw) // dv, (pool_width + 2 * keyw + gla_width) // dv
    assert pool_width % dk == 0 and (pool_width + 2 * keyw) % dv == 0
    gk = h_gate.shape[2]
    kern = functools.partial(_gla_kernel, seq=seq, dk=dk, dv=dv, chunk=chunk, rb=rb)
    row_dk = pltpu.VMEM((seq, dk), BF16)
    return pl.pallas_call(
        kern,
        grid=(b, heads),
        in_specs=[pl.BlockSpec((1, seq, dk), lambda i, h: (i, 0, q0 + h)),
                  pl.BlockSpec((1, seq, dk), lambda i, h: (i, 0, k0 + h)),
                  pl.BlockSpec((1, seq, dv), lambda i, h: (i, 0, v0 + h)),
                  pl.BlockSpec((1, seq, dv), lambda i, h: (i, 0, r0 + h)),
                  pl.BlockSpec((1, seq, gk), lambda i, h: (i, 0, 0)),
                  pl.BlockSpec((gk, dk), lambda i, h: (0, h)),
                  pl.BlockSpec((gk, dk), lambda i, h: (0, h)),
                  pl.BlockSpec((1, dk), lambda i, h: (0, h)),
                  pl.BlockSpec((1, dk), lambda i, h: (0, h)),
                  pl.BlockSpec((1, dv), lambda i, h: (0, h))],
        out_specs=pl.BlockSpec((1, seq, dv), lambda i, h: (i, 0, h)),
        out_shape=jax.ShapeDtypeStruct((b, seq, gla_width), BF16),
        scratch_shapes=[row_dk] * 10 + [
            pltpu.VMEM((seq // pair, dv, pair), BF16),
            pltpu.VMEM((seq // pair, 1, dk), F32),
            pltpu.VMEM((seq // pair, 1, dk), F32),
            pltpu.VMEM((seq, dv), F32),
            pltpu.VMEM((seq, dv), F32),
            pltpu.VMEM((dv, dk), F32),
            pltpu.VMEM((dv, dk), F32)],
        compiler_params=_cparams(("parallel", "arbitrary")),
        name="gla_mixer",
    )(h_main, h_main, h_main, h_main, h_gate, wgf, wgb, bgf, bgb, gla_norm_g)


ATTN_Q_SUBBLOCK = 128


def _diff_attn_kernel(q1_ref, q2_ref, k1_ref, k2_ref, v_ref, lq1_ref, lk1_ref, lq2_ref, lk2_ref, sg_ref,
                      o_ref, *, lam_init, qb):
    lam = (jnp.exp(jnp.sum(lq1_ref[...] * lk1_ref[...], axis=-1, keepdims=True))
           - jnp.exp(jnp.sum(lq2_ref[...] * lk2_ref[...], axis=-1, keepdims=True)) + lam_init)
    v = v_ref[0]
    k1 = k1_ref[0]
    k2 = k2_ref[0]

    def branch(q, k):
        s = _nt_dot(q, k)
        m = jnp.max(s, axis=-1, keepdims=True)
        p = jnp.exp(s - m)
        l = jnp.sum(p, axis=-1, keepdims=True)
        return jnp.dot(p.astype(BF16), v, preferred_element_type=F32) / l

    for c in range(q1_ref.shape[1] // qb):
        rows = slice(c * qb, (c + 1) * qb)
        o = branch(q1_ref[0, rows, :], k1) - lam * branch(q2_ref[0, rows, :], k2)
        rms = lax.rsqrt(jnp.mean(o * o, axis=-1, keepdims=True) + LN_EPS)
        o_ref[0, rows, :] = (o * rms * sg_ref[...] * (1.0 - lam_init)).astype(o_ref.dtype)


def _diff_attention(qkv, lam_q1, lam_k1, lam_q2, lam_k2, subln_g, lam_init, d_model, heads):
    b, seq, _ = qkv.shape
    hd = d_model // (2 * heads)
    tq = _tile(seq, 1024)
    k0 = d_model // hd
    v0 = 2 * d_model // (2 * hd)
    kern = functools.partial(_diff_attn_kernel, lam_init=lam_init, qb=_tile(tq, ATTN_Q_SUBBLOCK))
    vec = pl.BlockSpec((1, hd), lambda i, h, qi: (0, 0))
    return pl.pallas_call(
        kern,
        grid=(b, heads, seq // tq),
        in_specs=[pl.BlockSpec((1, tq, hd), lambda i, h, qi: (i, qi, 2 * h)),
                  pl.BlockSpec((1, tq, hd), lambda i, h, qi: (i, qi, 2 * h + 1)),
                  pl.BlockSpec((1, seq, hd), lambda i, h, qi: (i, 0, k0 + 2 * h)),
                  pl.BlockSpec((1, seq, hd), lambda i, h, qi: (i, 0, k0 + 2 * h + 1)),
                  pl.BlockSpec((1, seq, 2 * hd), lambda i, h, qi: (i, 0, v0 + h)),
                  vec, vec, vec, vec,
                  pl.BlockSpec((1, 2 * hd), lambda i, h, qi: (0, 0))],
        out_specs=pl.BlockSpec((1, tq, 2 * hd), lambda i, h, qi: (i, qi, h)),
        out_shape=jax.ShapeDtypeStruct((b, seq, d_model), BF16),
        compiler_params=_cparams(("parallel", "parallel", "arbitrary")),
        name="diff_attention",
    )(qkv, qkv, qkv, qkv, qkv, lam_q1, lam_k1, lam_q2, lam_k2, subln_g)


def _proj_ln_router_kernel(*refs, n_lhs, x_bounds, alpha, n_exp, top_k):
    lhs = refs[:n_lhs]
    w_ref = refs[n_lhs]
    x_refs = refs[n_lhs + 1:n_lhs + 1 + len(x_bounds)]
    rest = refs[n_lhs + 1 + len(x_bounds):]
    g_ref, b_ref, wr_ref, br_ref, tri_ref = rest[:5]
    x1_ref, idx_ref, gate_ref, rank_ref, cnt_ref = rest[5:10]
    run_ref = rest[10]

    @pl.when(pl.program_id(0) == 0)
    def _():
        run_ref[...] = jnp.zeros(run_ref.shape, F32)

    off = 0
    acc = None
    for l in lhs:
        kl = l.shape[1]
        part = jnp.dot(l[...], w_ref[off:off + kl, :], preferred_element_type=F32)
        acc = part if acc is None else acc + part
        off += kl
    x_res = _part_value(pl.program_id(0), x_refs, x_bounds)
    x1 = _layer_norm(alpha * x_res + acc, g_ref[...], b_ref[...])
    x1_ref[...] = x1

    tm = x1.shape[0]
    logits = _nt_dot(wr_ref[...], x1.astype(BF16)) + br_ref[...]
    eidx = lax.broadcasted_iota(I32, (n_exp, tm), 0)
    vals, idxs, hots = [], [], []
    for _ in range(top_k):
        m = jnp.max(logits, axis=0, keepdims=True)
        idx = jnp.min(jnp.where(logits == m, eidx, n_exp - 1), axis=0, keepdims=True)
        hot = eidx == idx
        vals.append(m)
        idxs.append(idx)
        hots.append(hot)
        logits = jnp.where(hot, -jnp.inf, logits)
    exps = [jnp.exp(v - vals[0]) for v in vals]
    den = exps[0]
    for e in exps[1:]:
        den = den + e
    gate_ref[...] = jnp.concatenate([e / den for e in exps], axis=0)
    idx_ref[...] = jnp.concatenate(idxs, axis=0)

    hot_all = jnp.concatenate([h.astype(BF16) for h in hots], axis=0)
    before = jnp.dot(hot_all, tri_ref[...], preferred_element_type=F32)
    run = run_ref[:, 0:1]
    ranks = []
    for kk in range(top_k):
        hf = hots[kk].astype(F32)
        ranks.append(jnp.sum(hf * (run + before[kk * n_exp:(kk + 1) * n_exp, :]), axis=0, keepdims=True))
        run = run + jnp.sum(hf, axis=1, keepdims=True)
    rank_ref[...] = jnp.concatenate(ranks, axis=0).astype(I32)
    run_ref[...] = jnp.broadcast_to(run, run_ref.shape)
    cnt_ref[...] = run_ref[...]


def _proj_ln_router(lhs_list, w, x_parts, ln_g, ln_b, wr_t, br, tri, alpha, tm):
    n, d = sum(p.shape[0] for p in x_parts), x_parts[0].shape[1]
    n_exp = wr_t.shape[0]
    x_bounds = _part_bounds(x_parts, tm)
    kern = functools.partial(_proj_ln_router_kernel, n_lhs=len(lhs_list), x_bounds=x_bounds, alpha=alpha,
                             n_exp=n_exp, top_k=TOP_K)
    row = lambda i: (i, 0)
    fixed = lambda i: (0, 0)
    in_specs = [pl.BlockSpec((tm, l.shape[1]), row) for l in lhs_list] + [
        pl.BlockSpec(w.shape, fixed)] + _part_specs(x_parts, tm, x_bounds) + [
        pl.BlockSpec((1, d), fixed), pl.BlockSpec((1, d), fixed),
        pl.BlockSpec((n_exp, d), fixed), pl.BlockSpec((n_exp, 1), fixed), pl.BlockSpec((tm, tm), fixed)]
    tok = lambda i: (0, i)
    return pl.pallas_call(
        kern,
        grid=(n // tm,),
        in_specs=in_specs,
        out_specs=[pl.BlockSpec((tm, d), row), pl.BlockSpec((TOP_K, tm), tok), pl.BlockSpec((TOP_K, tm), tok),
                   pl.BlockSpec((TOP_K, tm), tok), pl.BlockSpec((n_exp, LANES), fixed)],
        out_shape=[jax.ShapeDtypeStruct((n, d), F32), jax.ShapeDtypeStruct((TOP_K, n), I32),
                   jax.ShapeDtypeStruct((TOP_K, n), F32), jax.ShapeDtypeStruct((TOP_K, n), I32),
                   jax.ShapeDtypeStruct((n_exp, LANES), F32)],
        scratch_shapes=[pltpu.VMEM((n_exp, LANES), F32)],
        compiler_params=_cparams(("arbitrary",)),
        name="proj_ln_router",
    )(*lhs_list, w, *x_parts, ln_g, ln_b, wr_t, br, tri)


def _dispatch_kernel(dest_ref, x_ref, xs_in_ref, xs_ref, xp_ref, sem, *, top_k):
    del xs_in_ref
    tm, d = x_ref.shape
    x = x_ref[...]
    lo = lax.bitcast_convert_type(x[:, :d // 2].astype(BF16).astype(F32), U32)
    hi = lax.bitcast_convert_type(x[:, d // 2:].astype(BF16).astype(F32), U32)
    xp_ref[...] = jnp.bitwise_or(lax.shift_right_logical(lo, jnp.uint32(16)),
                                 jnp.bitwise_and(hi, jnp.uint32(0xFFFF0000))).reshape(xp_ref.shape)

    for kk in range(top_k):
        def issue(g, carry, kk=kk):
            for r in range(SUBLANES):
                dst = xs_ref.at[pl.ds(dest_ref[kk * tm + g * SUBLANES + r], 1)]
                pltpu.make_async_copy(xp_ref.at[g, pl.ds(r, 1)], dst, sem).start(priority=r % DMA_PRIORITIES)
            return carry

        lax.fori_loop(0, tm // SUBLANES, issue, 0)
    for _ in range(top_k):
        pltpu.make_async_copy(xs_ref.at[pl.ds(0, tm)], xs_ref.at[pl.ds(0, tm)], sem).wait()


def _dispatch(dest_flat, x1, xs_zero, tm):
    n, d = x1.shape
    kern = functools.partial(_dispatch_kernel, top_k=TOP_K)
    return pl.pallas_call(
        kern,
        grid=(n // tm,),
        in_specs=[pl.BlockSpec((TOP_K * tm,), lambda i: (i,), memory_space=pltpu.SMEM),
                  pl.BlockSpec((tm, d), lambda i: (i, 0)),
                  pl.BlockSpec(memory_space=pl.ANY)],
        out_specs=pl.BlockSpec(memory_space=pl.ANY),
        out_shape=jax.ShapeDtypeStruct(xs_zero.shape, U32),
        scratch_shapes=[pltpu.VMEM((tm // SUBLANES, SUBLANES, d // 2), U32), pltpu.SemaphoreType.DMA(())],
        input_output_aliases={2: 0},
        compiler_params=_cparams(("arbitrary",)),
        name="moe_dispatch",
    )(dest_flat, x1, xs_zero)


def _expert_kernel(be_ref, nv_ref, xs_ref, wg_ref, wu_ref, bg_ref, bu_ref, wd_ref, bd_ref, y_ref, xb_ref, *, tn):
    del be_ref
    i = pl.program_id(0)
    half = xs_ref.shape[1]
    d_ff = wg_ref.shape[2]

    @pl.when(i < nv_ref[0])
    def _():
        u = xs_ref[...]
        lo = lax.bitcast_convert_type(lax.shift_left(u, jnp.uint32(16)), F32)
        hi = lax.bitcast_convert_type(jnp.bitwise_and(u, jnp.uint32(0xFFFF0000)), F32)
        xb_ref[:, :half] = lo.astype(BF16)
        xb_ref[:, half:] = hi.astype(BF16)
        xb = xb_ref[...]
        acc = None
        for c in range(d_ff // tn):
            cols = slice(c * tn, (c + 1) * tn)
            hg = jnp.dot(xb, wg_ref[0, :, cols], preferred_element_type=F32) + bg_ref[0, :, cols]
            hu = jnp.dot(xb, wu_ref[0, :, cols], preferred_element_type=F32) + bu_ref[0, :, cols]
            gate = jnp.minimum(hg, SWIGLU_LIMIT)
            up = jnp.clip(hu, -SWIGLU_LIMIT, SWIGLU_LIMIT)
            glu = gate / (1.0 + jnp.exp(-SWIGLU_ALPHA * gate))
            part = jnp.dot(((up + 1.0) * glu).astype(BF16), wd_ref[0, cols, :], preferred_element_type=F32)
            acc = part if acc is None else acc + part
        y_ref[...] = acc + bd_ref[0]

    @pl.when(i >= nv_ref[0])
    def _():
        y_ref[...] = jnp.broadcast_to(bd_ref[0], y_ref.shape)


def _expert_ffn(block_e, n_valid, xs, wg, wu, bg, bu, wd, bd, tm, tn):
    p, half = xs.shape
    d = 2 * half
    d_ff = wg.shape[2]

    def blk(i, nv):
        return jnp.minimum(i, nv[0] - 1)

    def expert(shape):
        return pl.BlockSpec(shape, lambda i, be, nv: (be[blk(i, nv)], 0, 0), pipeline_mode=pl.Buffered(1))

    grid_spec = pltpu.PrefetchScalarGridSpec(
        num_scalar_prefetch=2,
        grid=(p // tm,),
        in_specs=[pl.BlockSpec((tm, half), lambda i, be, nv: (blk(i, nv), 0)),
                  expert((1, d, d_ff)), expert((1, d, d_ff)), expert((1, 1, d_ff)), expert((1, 1, d_ff)),
                  expert((1, d_ff, d)), expert((1, 1, d))],
        out_specs=pl.BlockSpec((tm, d), lambda i, be, nv: (i, 0)),
        scratch_shapes=[pltpu.VMEM((tm, d), BF16)])
    return pl.pallas_call(
        functools.partial(_expert_kernel, tn=tn),
        grid_spec=grid_spec,
        out_shape=jax.ShapeDtypeStruct((p, d), F32),
        compiler_params=_cparams(("arbitrary",)),
        name="moe_experts",
    )(block_e, n_valid, xs, wg, wu, bg, bu, wd, bd)


def _combine_kernel(dest_ref, x_ref, gates_ref, g_ref, b_ref, y_ref, y3_ref, *refs, top_k, alpha, out_bounds):
    o_refs = refs[:len(out_bounds)]
    buf_ref, sem = refs[len(out_bounds):]
    tm, d = x_ref.shape

    for kk in range(top_k):
        def issue(g, carry, kk=kk):
            for r in range(SUBLANES):
                src = y_ref.at[pl.ds(dest_ref[kk * tm + g * SUBLANES + r], 1)]
                pltpu.make_async_copy(src, buf_ref.at[kk, g, pl.ds(r, 1)], sem).start(priority=r % DMA_PRIORITIES)
            return carry

        lax.fori_loop(0, tm // SUBLANES, issue, 0)
    for kk in range(top_k):
        pltpu.make_async_copy(y3_ref.at[pl.ds(0, tm // SUBLANES)], buf_ref.at[kk], sem).wait()
    ff = gates_ref[:, 0:1] * buf_ref[0].reshape(tm, d)
    for kk in range(1, top_k):
        ff = ff + gates_ref[:, kk:kk + 1] * buf_ref[kk].reshape(tm, d)
    res = _layer_norm(alpha * x_ref[...] + ff, g_ref[...], b_ref[...])
    if len(o_refs) == 1:
        o_refs[0][...] = res
    else:
        i = pl.program_id(0)
        for o_ref, (lo, hi) in zip(o_refs, out_bounds):
            @pl.when(jnp.logical_and(i >= lo, i < hi))
            def _(o_ref=o_ref):
                o_ref[...] = res


def _combine(dest_flat, x1, gates_t, ln_g, ln_b, y, alpha, tm, out_rows):
    n, d = x1.shape
    assert sum(out_rows) == n
    outs = [jax.ShapeDtypeStruct((r, d), F32) for r in out_rows]
    out_bounds = _part_bounds(outs, tm)
    kern = functools.partial(_combine_kernel, top_k=TOP_K, alpha=alpha, out_bounds=out_bounds)
    return pl.pallas_call(
        kern,
        grid=(n // tm,),
        in_specs=[pl.BlockSpec((TOP_K * tm,), lambda i: (i,), memory_space=pltpu.SMEM),
                  pl.BlockSpec((tm, d), lambda i: (i, 0)),
                  pl.BlockSpec((tm, TOP_K), lambda i: (i, 0)),
                  pl.BlockSpec((1, d), lambda i: (0, 0)),
                  pl.BlockSpec((1, d), lambda i: (0, 0)),
                  pl.BlockSpec(memory_space=pl.ANY),
                  pl.BlockSpec(memory_space=pl.ANY)],
        out_specs=_part_specs(outs, tm, out_bounds),
        out_shape=outs,
        scratch_shapes=[pltpu.VMEM((TOP_K, tm // SUBLANES, SUBLANES, d), F32), pltpu.SemaphoreType.DMA(())],
        compiler_params=_cparams(("arbitrary",)),
        name="moe_combine",
    )(dest_flat, x1, gates_t, ln_g, ln_b, y, y.reshape(y.shape[0] // SUBLANES, SUBLANES, d))


MOE_ROW_TILE = 512
MOE_FF_TILE = 512
MOE_TOKEN_TILE = 256
ROUTER_TILE = 256


def _moe_layer(x1, idx, gates, rank, cnt, layer, ew, ln_g, ln_b, alpha, xs_buf, out_rows):
    n, d = x1.shape
    n_exp = cnt.shape[0]
    tm_e = _tile(n * TOP_K, MOE_ROW_TILE)
    tok = _tile(n, MOE_TOKEN_TILE)
    counts = cnt[:, 0].astype(I32)
    padded = ((counts + tm_e - 1) // tm_e) * tm_e
    pend = jnp.cumsum(padded)
    pstart = pend - padded
    dest = rank
    for e in range(n_exp):
        dest = dest + jnp.where(idx == e, pstart[e], 0)
    n_blocks = (n * TOP_K) // tm_e + n_exp
    starts = jnp.arange(n_blocks, dtype=I32) * tm_e
    block_e = jnp.minimum(jnp.sum((starts[:, None] >= pend[None, :]).astype(I32), axis=1), n_exp - 1) + layer * n_exp
    n_valid = (pend[-1:] // tm_e).astype(I32)
    dest_flat = dest.reshape(TOP_K, n // tok, tok).transpose(1, 0, 2).reshape(-1)
    if xs_buf is None:
        xs_buf = jnp.zeros((n_blocks * tm_e, d // 2), U32)
    xs = _dispatch(dest_flat, x1, xs_buf, tok)
    y = _expert_ffn(block_e, n_valid, xs, *ew, tm_e, _tile(ew[0].shape[2], MOE_FF_TILE))
    return _combine(dest_flat, x1, gates.T, ln_g, ln_b, y, alpha, tok, out_rows), xs


def kernel(x_prompt, x_sample, e_w_in, e_w_pool, e_pool_scale, e_w_gate2, e_b_gate2, e_gla_norm_g, e_w_out,
           o_w_qkv, o_lam_q1, o_lam_k1, o_lam_q2, o_lam_k2, o_subln_g, o_w_out,
           ln1_g, ln1_b, ln2_g, ln2_b,
           moe_w_router, moe_b_router, moe_w_gate_up, moe_b_gate_up, moe_w_down, moe_b_down):
    depth = ln1_g.shape[0]
    alpha = (2.0 * depth) ** 0.25
    nb_p, seq, d = x_prompt.shape
    assert x_sample.shape[1:] == (seq, d)
    bsz = nb_p + x_sample.shape[0]
    n = bsz * seq
    x = [x_prompt.reshape(nb_p * seq, d), x_sample.reshape(n - nb_p * seq, d)]
    final_rows = tuple(p.shape[0] for p in x)

    n_exp = moe_w_router.shape[2]
    d_ff = moe_w_down.shape[2]
    w_gate, w_up = _deinterleave_bf16(moe_w_gate_up.reshape(depth * n_exp, d, 2 * d_ff))
    ew = (w_gate, w_up,
          moe_b_gate_up[..., 0::2].reshape(depth * n_exp, 1, d_ff),
          moe_b_gate_up[..., 1::2].reshape(depth * n_exp, 1, d_ff),
          _cast_bf16(moe_w_down.reshape(depth * n_exp, d_ff, d)),
          moe_b_down.reshape(depth * n_exp, 1, d))
    wr_t = jnp.swapaxes(moe_w_router, 1, 2).astype(BF16)
    tm_r = _tile(n, ROUTER_TILE)
    tri = (jnp.arange(tm_r)[:, None] < jnp.arange(tm_r)[None, :]).astype(BF16)

    pool_width = e_w_pool.shape[1] * e_w_pool.shape[2]
    gla_width = d - pool_width
    keyw = gla_width // 2
    main_w = pool_width + 2 * keyw + 2 * gla_width
    hd = d // (2 * DIFF_HEADS)

    xs_buf = None
    for l in range(depth):
        i = l // 2
        if l % 2 == 0:
            w_in = e_w_in[i]
            h_main = _matmul(x, w_in[:, :main_w].astype(BF16), F32, "in_proj").reshape(bsz, seq, main_w)
            w_gl = jnp.pad(w_in[:, main_w:], ((0, 0), (0, LANES - 2 * GLA_GATE_RANK))).astype(BF16)
            h_gate = _matmul(x, w_gl, F32, "in_proj_gates").reshape(bsz, seq, LANES)
            a_out = _pool_mixer(h_main, e_w_pool[i].astype(BF16), e_pool_scale[i][None, :], pool_width)
            wg2 = e_w_gate2[i].astype(BF16)
            zpad = jnp.zeros((LANES - 2 * GLA_GATE_RANK, keyw), BF16)
            zrank = jnp.zeros((GLA_GATE_RANK, keyw), BF16)
            wgf = jnp.concatenate([wg2[0], zrank, zpad], axis=0)
            wgb = jnp.concatenate([zrank, wg2[1], zpad], axis=0)
            b_out = _gla_mixer(h_main, h_gate, wgf, wgb, e_b_gate2[i][0][None, :], e_b_gate2[i][1][None, :],
                               e_gla_norm_g[i][None, :], pool_width, GLA_HEADS)
            lhs = [a_out.reshape(n, pool_width), b_out.reshape(n, gla_width)]
            w_out = e_w_out[i].astype(BF16)
        else:
            lam_init = 0.8 - 0.6 * math.exp(-0.3 * l)
            cos, sin = _rope_tables(seq, hd)
            x_all = x[0] if len(x) == 1 else jnp.concatenate(x, axis=0)
            qkv = _qkv_proj(x_all, o_w_qkv[i].astype(BF16), cos, sin, seq, d, hd).reshape(bsz, seq, 3 * d)
            attn = _diff_attention(qkv, o_lam_q1[i][None, :], o_lam_k1[i][None, :], o_lam_q2[i][None, :],
                                   o_lam_k2[i][None, :], o_subln_g[i][None, :], lam_init, d, DIFF_HEADS)
            lhs = [attn.reshape(n, d)]
            w_out = o_w_out[i].astype(BF16)
        x1, idx, gates, rank, cnt = _proj_ln_router(
            lhs, w_out, x, ln1_g[l][None, :], ln1_b[l][None, :], wr_t[l], moe_b_router[l][:, None], tri,
            alpha, tm_r)
        out_rows = final_rows if l == depth - 1 else (n,)
        x, xs_buf = _moe_layer(x1, idx, gates, rank, cnt, l, ew, ln2_g[l][None, :], ln2_b[l][None, :], alpha,
                               xs_buf, out_rows)

    return (x[0].reshape(x_prompt.shape), x[1].reshape(x_sample.shape))
```

```python
import functools
import math

import jax
import jax.numpy as jnp
from jax import lax
from jax.experimental import pallas as pl
from jax.experimental.pallas import tpu as pltpu

F32 = jnp.float32
BF16 = jnp.bfloat16
I32 = jnp.int32
U32 = jnp.uint32

POOL_WINDOWS = (2, 4, 8, 16)
GLA_HEADS = 4
GLA_GATE_RANK = 16
GLA_GATE_NORM = 16.0
GLA_CHUNK = 64
DIFF_HEADS = 8
ROPE_THETA = 10000.0
TOP_K = 4
SWIGLU_LIMIT = 7.0
SWIGLU_ALPHA = 1.702
LN_EPS = 1e-5

V7X_VMEM_BYTES = 64 * 2 ** 20
VMEM_LIMIT = V7X_VMEM_BYTES - 8 * 2 ** 20
LANES = 128
SUBLANES = 8
SUBLANE_SHIFT = 3
DMA_PRIORITIES = 2


def _tile(n, pref):
    t = min(n, pref)
    while n % t:
        t //= 2
    return t


def _cparams(sem):
    return pltpu.CompilerParams(dimension_semantics=sem, vmem_limit_bytes=VMEM_LIMIT)


def _nt_dot(a, b):
    return lax.dot_general(a, b, (((1,), (1,)), ((), ())), preferred_element_type=F32)


def _layer_norm(y, g, b):
    mu = jnp.mean(y, axis=-1, keepdims=True)
    yc = y - mu
    var = jnp.mean(yc * yc, axis=-1, keepdims=True)
    return yc * lax.rsqrt(var + LN_EPS) * g + b


def _part_bounds(parts, tm):
    bounds, lo = [], 0
    for p in parts:
        assert p.shape[0] % tm == 0, (p.shape, tm)
        bounds.append((lo, lo + p.shape[0] // tm))
        lo = bounds[-1][1]
    return tuple(bounds)


def _part_specs(parts, tm, bounds):
    def spec(p, lo, hi):
        return pl.BlockSpec((tm, p.shape[1]), lambda i, *_: (jnp.clip(i - lo, 0, hi - lo - 1), 0))
    return [spec(p, lo, hi) for p, (lo, hi) in zip(parts, bounds)]


def _part_value(i, refs, bounds, rows=slice(None)):
    val = refs[-1][rows, :]
    for ref, (_, hi) in zip(reversed(refs[:-1]), reversed(bounds[:-1])):
        val = jnp.where(i < hi, ref[rows, :], val)
    return val


def _mm_kernel(*refs, bounds):
    x_refs = refs[:len(bounds)]
    w_ref, o_ref, xb_ref = refs[len(bounds):]

    @pl.when(pl.program_id(1) == 0)
    def _():
        xb_ref[...] = _part_value(pl.program_id(0), x_refs, bounds).astype(BF16)

    o_ref[...] = jnp.dot(xb_ref[...], w_ref[...], preferred_element_type=F32).astype(o_ref.dtype)


def _matmul(x_parts, w, out_dtype, name, tm_pref=1024, tn_pref=1024):
    m = sum(p.shape[0] for p in x_parts)
    k, n = w.shape
    tm = _tile(math.gcd(*[p.shape[0] for p in x_parts]), tm_pref)
    tn = _tile(n, tn_pref)
    bounds = _part_bounds(x_parts, tm)
    return pl.pallas_call(
        functools.partial(_mm_kernel, bounds=bounds),
        grid=(m // tm, n // tn),
        in_specs=_part_specs(x_parts, tm, bounds) + [pl.BlockSpec((k, tn), lambda i, j: (0, j))],
        out_specs=pl.BlockSpec((tm, tn), lambda i, j: (i, j)),
        out_shape=jax.ShapeDtypeStruct((m, n), out_dtype),
        scratch_shapes=[pltpu.VMEM((tm, k), BF16)],
        compiler_params=_cparams(("parallel", "arbitrary")),
        name=name,
    )(*x_parts, w)


MXU_TILE = 256


def _cast_kernel(x_ref, o_ref):
    o_ref[...] = x_ref[...].astype(o_ref.dtype)


def _cast_bf16(x):
    a, r, c = x.shape
    tr = _tile(r, 512)
    return pl.pallas_call(
        _cast_kernel,
        grid=(a, r // tr),
        in_specs=[pl.BlockSpec((1, tr, c), lambda i, j: (i, j, 0))],
        out_specs=pl.BlockSpec((1, tr, c), lambda i, j: (i, j, 0)),
        out_shape=jax.ShapeDtypeStruct(x.shape, BF16),
        compiler_params=_cparams(("parallel", "parallel")),
        name="cast_bf16",
    )(x)


def _deinterleave_kernel(w_ref, p_ref, g_ref, u_ref):
    w = w_ref[0].astype(BF16)
    hm = MXU_TILE // 2
    for c in range(w.shape[1] // MXU_TILE):
        r = jnp.dot(w[:, c * MXU_TILE:(c + 1) * MXU_TILE], p_ref[...], preferred_element_type=F32).astype(BF16)
        g_ref[0, :, c * hm:(c + 1) * hm] = r[:, :hm]
        u_ref[0, :, c * hm:(c + 1) * hm] = r[:, hm:]


def _deinterleave_bf16(w):
    a, r, c2 = w.shape
    tr, tc2 = _tile(r, 1024), _tile(c2, 1024)
    assert tc2 % MXU_TILE == 0
    rows = jnp.arange(MXU_TILE)
    perm = (jnp.arange(MXU_TILE)[None, :] == (rows // 2 + (rows % 2) * (MXU_TILE // 2))[:, None]).astype(BF16)
    out = jax.ShapeDtypeStruct((a, r, c2 // 2), BF16)
    return pl.pallas_call(
        _deinterleave_kernel,
        grid=(a, r // tr, c2 // tc2),
        in_specs=[pl.BlockSpec((1, tr, tc2), lambda i, j, k: (i, j, k)),
                  pl.BlockSpec((MXU_TILE, MXU_TILE), lambda i, j, k: (0, 0))],
        out_specs=[pl.BlockSpec((1, tr, tc2 // 2), lambda i, j, k: (i, j, k))] * 2,
        out_shape=[out, out],
        compiler_params=_cparams(("parallel", "parallel", "parallel")),
        name="deinterleave_bf16",
    )(w, perm)


def _rope_table_kernel(cos_ref, sin_ref, *, half, theta):
    rows, hd = cos_ref.shape
    pos = (lax.broadcasted_iota(I32, (rows, hd), 0) + pl.program_id(0) * rows).astype(F32)
    lane = lax.broadcasted_iota(I32, (rows, hd), 1)
    fi = jnp.where(lane >= half, lane - half, lane).astype(F32)
    inv = jnp.exp(fi * (-math.log(theta) / half))
    ang = pos * inv
    cos_ref[...] = jnp.cos(ang)
    s = jnp.sin(ang)
    sin_ref[...] = jnp.where(lane < half, -s, s)


def _rope_tables(seq, hd):
    rows = _tile(seq, 256)
    return pl.pallas_call(
        functools.partial(_rope_table_kernel, half=hd // 2, theta=ROPE_THETA),
        grid=(seq // rows,),
        out_specs=[pl.BlockSpec((rows, hd), lambda i: (i, 0))] * 2,
        out_shape=[jax.ShapeDtypeStruct((seq, hd), F32)] * 2,
        compiler_params=_cparams(("parallel",)),
        name="rope_tables",
    )()


def _qkv_kernel(x_ref, w_ref, cos_ref, sin_ref, o_ref, xb_ref, *, n_q_tiles, n_rope_tiles, hd, q_scale):
    j = pl.program_id(1)

    @pl.when(j == 0)
    def _():
        xb_ref[...] = x_ref[...].astype(BF16)

    acc = jnp.dot(xb_ref[...], w_ref[...], preferred_element_type=F32)
    tn = acc.shape[1]

    rot = jnp.where(j < n_rope_tiles, jnp.where(j < n_q_tiles, q_scale, 1.0), 0.0)
    cos = cos_ref[...] * rot + jnp.where(j < n_rope_tiles, 0.0, 1.0)
    sin = sin_ref[...] * rot
    for h in range(tn // hd):
        t = acc[:, h * hd:(h + 1) * hd]
        r = t * cos + pltpu.roll(t, hd // 2, axis=1) * sin
        o_ref[:, h * hd:(h + 1) * hd] = r.astype(o_ref.dtype)


def _qkv_proj(x, w, cos, sin, seq, d_model, hd):
    m, k = x.shape
    n = w.shape[1]
    tm, tn = _tile(seq, 1024), _tile(d_model, 1024)
    spb = seq // tm
    kern = functools.partial(_qkv_kernel, n_q_tiles=d_model // tn, n_rope_tiles=2 * d_model // tn,
                             hd=hd, q_scale=hd ** -0.5)
    return pl.pallas_call(
        kern,
        grid=(m // tm, n // tn),
        in_specs=[pl.BlockSpec((tm, k), lambda i, j: (i, 0)),
                  pl.BlockSpec((k, tn), lambda i, j: (0, j)),
                  pl.BlockSpec((tm, hd), lambda i, j: (i % spb, 0)),
                  pl.BlockSpec((tm, hd), lambda i, j: (i % spb, 0))],
        out_specs=pl.BlockSpec((tm, tn), lambda i, j: (i, j)),
        out_shape=jax.ShapeDtypeStruct((m, n), BF16),
        scratch_shapes=[pltpu.VMEM((tm, k), BF16)],
        compiler_params=_cparams(("parallel", "arbitrary")),
        name="qkv_rope",
    )(x, w, cos, sin)


POOL_HALO = 8


def _pool_kernel(u_ref, w_ref, sc_ref, o_ref, pad_ref, *, windows, gw, seq, rc):
    assert max(windows) // 2 <= POOL_HALO
    pad_ref[0:POOL_HALO, :] = jnp.zeros((POOL_HALO, gw), F32)
    pad_ref[POOL_HALO + seq:POOL_HALO + seq + POOL_HALO, :] = jnp.zeros((POOL_HALO, gw), F32)
    for g, w in enumerate(windows):
        hw = w // 2
        pad_ref[POOL_HALO:POOL_HALO + seq, :] = u_ref[0, :, g * gw:(g + 1) * gw]
        for c in range(seq // rc):
            base = POOL_HALO + c * rc
            acc = pad_ref[base - hw:base - hw + rc, :]
            for off in range(-hw + 1, hw):
                acc = acc + pad_ref[base + off:base + off + rc, :]
            pos = lax.broadcasted_iota(I32, (rc, gw), 0) + c * rc
            cnt = (jnp.minimum(pos + hw, seq) - jnp.maximum(pos - hw, 0)).astype(F32)
            pooled = acc / cnt - pad_ref[base:base + rc, :]
            mixed = jnp.dot(pooled.astype(BF16), w_ref[g], preferred_element_type=F32)
            mixed = mixed * sc_ref[:, g * gw:(g + 1) * gw]
            o_ref[0, c * rc:(c + 1) * rc, g * gw:(g + 1) * gw] = mixed.astype(o_ref.dtype)


def _pool_mixer(h_main, w_pool, pool_scale, pool_width):
    b, seq, _ = h_main.shape
    g = w_pool.shape[0]
    gw = pool_width // g
    rc = _tile(seq, 256)
    kern = functools.partial(_pool_kernel, windows=POOL_WINDOWS, gw=gw, seq=seq, rc=rc)
    return pl.pallas_call(
        kern,
        grid=(b,),
        in_specs=[pl.BlockSpec((1, seq, pool_width), lambda i: (i, 0, 0)),
                  pl.BlockSpec((g, gw, gw), lambda i: (0, 0, 0)),
                  pl.BlockSpec((1, pool_width), lambda i: (0, 0))],
        out_specs=pl.BlockSpec((1, seq, pool_width), lambda i: (i, 0, 0)),
        out_shape=jax.ShapeDtypeStruct((b, seq, pool_width), BF16),
        scratch_shapes=[pltpu.VMEM((seq + 2 * POOL_HALO, gw), F32)],
        compiler_params=_cparams(("parallel",)),
        name="pool_mixer",
    )(h_main, w_pool, pool_scale)


def _log_sigmoid(x):
    return jnp.minimum(x, 0.0) - jnp.log(1.0 + jnp.exp(-jnp.abs(x)))


def _gla_kernel(q_ref, k_ref, v_ref, r_ref, gl_ref, wgf_ref, wgb_ref, bgf_ref, bgb_ref, gn_ref, o_ref,
                qm_f, km_f, kd_f, qx_f, kx_f, qm_b, km_b, kd_b, qx_b, kx_b,
                vt_ref, dec_f, dec_b, of_ref, ob_ref, sf_ref, sb_ref, *, seq, dk, dv, chunk, rb):
    pair = 2 * chunk
    n_pairs = seq // pair
    q_scale = dk ** -0.5

    def prep(blk, carry):
        r0 = pl.multiple_of(blk * rb, rb)
        rows = pl.ds(r0, rb)
        q = q_ref[0, rows, :] * q_scale
        k = k_ref[0, rows, :]
        gl = gl_ref[0, rows, :].astype(BF16)
        row = lax.broadcasted_iota(I32, (rb, dk), 0)
        rc = jnp.bitwise_and(row, chunk - 1)
        second = jnp.bitwise_and(row, chunk) != 0

        def scans(pre):
            g = _log_sigmoid(pre) / GLA_GATE_NORM
            pfx = g
            sfx = g
            sh = 1
            while sh < chunk:
                pfx = pfx + jnp.where(rc >= sh, pltpu.roll(pfx, sh, axis=0), 0.0)
                sfx = sfx + jnp.where(rc + sh < chunk, pltpu.roll(sfx, rb - sh, axis=0), 0.0)
                sh *= 2
            tot = pfx + sfx - g
            return pfx, sfx, tot

        def emit(bcum, tot, q_extra_rows, qm, km, kd, qx, kx, dec):
            tot_prev = pltpu.roll(tot, chunk, axis=0)
            tot_next = pltpu.roll(tot, rb - chunk, axis=0)
            other = jnp.where(second, tot_prev, tot_next)
            qm[rows, :] = (q * jnp.exp(bcum)).astype(BF16)
            km[rows, :] = (k * jnp.exp(-bcum)).astype(BF16)
            kd[rows, :] = (k * jnp.exp(tot - bcum)).astype(BF16)
            qx[rows, :] = (q * jnp.exp(bcum + jnp.where(q_extra_rows, other, 0.0))).astype(BF16)
            kx[rows, :] = (k * jnp.exp(tot - bcum + jnp.where(q_extra_rows, 0.0, other))).astype(BF16)
            d = jnp.exp(tot + other)
            for p in range(rb // pair):
                dec[blk * (rb // pair) + p] = d[p * pair:p * pair + 1, :]

        pre_f = jnp.dot(gl, wgf_ref[...], preferred_element_type=F32) + bgf_ref[...]
        pfx, _, tot = scans(pre_f)
        emit(pfx, tot, second, qm_f, km_f, kd_f, qx_f, kx_f, dec_f)
        pre_b = jnp.dot(gl, wgb_ref[...], preferred_element_type=F32) + bgb_ref[...]
        _, sfx, tot = scans(pre_b)
        emit(sfx, tot, jnp.logical_not(second), qm_b, km_b, kd_b, qx_b, kx_b, dec_b)
        return carry

    lax.fori_loop(0, seq // rb, prep, 0)

    for p in range(n_pairs):
        vt_ref[p] = v_ref[0, p * pair:(p + 1) * pair, :].T.astype(BF16)

    ri = lax.broadcasted_iota(I32, (pair, pair), 0)
    ci = lax.broadcasted_iota(I32, (pair, pair), 1)
    same = (ri >= chunk) == (ci >= chunk)
    m1_f = jnp.logical_and(same, ci <= ri)
    m2_f = jnp.logical_and(ri >= chunk, ci < chunk)
    m1_b = jnp.logical_and(same, ci >= ri)
    m2_b = jnp.logical_and(ri < chunk, ci >= chunk)

    sf_ref[...] = jnp.zeros((dv, dk), F32)
    sb_ref[...] = jnp.zeros((dv, dk), F32)

    def step(p, qm, km, kd, qx, kx, dec, s_ref, out, m1, m2):
        rows = pl.ds(pl.multiple_of(p * pair, pair), pair)
        a1 = _nt_dot(qm[rows, :], km[rows, :])
        a2 = _nt_dot(qm[rows, :], kd[rows, :])
        att = jnp.where(m1, a1, jnp.where(m2, a2, 0.0)).astype(BF16)
        vb = v_ref[0, rows, :].astype(BF16)
        st = s_ref[...]
        o = jnp.dot(att, vb, preferred_element_type=F32) + _nt_dot(qx[rows, :], st.astype(BF16))
        out[rows, :] = o
        s_ref[...] = st * dec[p] + jnp.dot(vt_ref[p], kx[rows, :], preferred_element_type=F32)

    def scan(i, carry):
        step(i, qm_f, km_f, kd_f, qx_f, kx_f, dec_f, sf_ref, of_ref, m1_f, m2_f)
        step(n_pairs - 1 - i, qm_b, km_b, kd_b, qx_b, kx_b, dec_b, sb_ref, ob_ref, m1_b, m2_b)
        return carry

    lax.fori_loop(0, n_pairs, scan, 0)

    def finish(blk, carry):
        rows = pl.ds(pl.multiple_of(blk * rb, rb), rb)
        o = of_ref[rows, :] + ob_ref[rows, :]
        mu = jnp.mean(o, axis=-1, keepdims=True)
        oc = o - mu
        var = jnp.mean(oc * oc, axis=-1, keepdims=True)
        on = oc * lax.rsqrt(var + LN_EPS) * gn_ref[...]
        r = r_ref[0, rows, :]
        silu = r / (1.0 + jnp.exp(-r))
        o_ref[0, rows, :] = (silu * on).astype(o_ref.dtype)
        return carry

    lax.fori_loop(0, seq // rb, finish, 0)


def _gla_mixer(h_main, h_gate, wgf, wgb, bgf, bgb, gla_norm_g, pool_width, heads):
    b, seq, _ = h_main.shape
    gla_width = gla_norm_g.shape[1]
    dv = gla_width // heads
    dk = dv // 2
    keyw = heads * dk
    chunk = GLA_CHUNK
    pair = 2 * chunk
    assert seq % pair == 0 and chunk & (chunk - 1) == 0
    rb = _tile(seq, 256)
    assert rb % pair == 0
    q0, k0 = pool_width // dk, (pool_width + keyw) // dk
    v0, r0 = (pool_width + 2 * keyw) // dv, (pool_width + 2 * keyw + gla_width) // dv
    assert pool_width % dk == 0 and (pool_width + 2 * keyw) % dv == 0
    gk = h_gate.shape[2]
    kern = functools.partial(_gla_kernel, seq=seq, dk=dk, dv=dv, chunk=chunk, rb=rb)
    row_dk = pltpu.VMEM((seq, dk), BF16)
    return pl.pallas_call(
        kern,
        grid=(b, heads),
        in_specs=[pl.BlockSpec((1, seq, dk), lambda i, h: (i, 0, q0 + h)),
                  pl.BlockSpec((1, seq, dk), lambda i, h: (i, 0, k0 + h)),
                  pl.BlockSpec((1, seq, dv), lambda i, h: (i, 0, v0 + h)),
                  pl.BlockSpec((1, seq, dv), lambda i, h: (i, 0, r0 + h)),
                  pl.BlockSpec((1, seq, gk), lambda i, h: (i, 0, 0)),
                  pl.BlockSpec((gk, dk), lambda i, h: (0, h)),
                  pl.BlockSpec((gk, dk), lambda i, h: (0, h)),
                  pl.BlockSpec((1, dk), lambda i, h: (0, h)),
                  pl.BlockSpec((1, dk), lambda i, h: (0, h)),
                  pl.BlockSpec((1, dv), lambda i, h: (0, h))],
        out_specs=pl.BlockSpec((1, seq, dv), lambda i, h: (i, 0, h)),
        out_shape=jax.ShapeDtypeStruct((b, seq, gla_width), BF16),
        scratch_shapes=[row_dk] * 10 + [
            pltpu.VMEM((seq // pair, dv, pair), BF16),
            pltpu.VMEM((seq // pair, 1, dk), F32),
            pltpu.VMEM((seq // pair, 1, dk), F32),
            pltpu.VMEM((seq, dv), F32),
            pltpu.VMEM((seq, dv), F32),
            pltpu.VMEM((dv, dk), F32),
            pltpu.VMEM((dv, dk), F32)],
        compiler_params=_cparams(("parallel", "arbitrary")),
        name="gla_mixer",
    )(h_main, h_main, h_main, h_main, h_gate, wgf, wgb, bgf, bgb, gla_norm_g)


ATTN_Q_SUBBLOCK = 128


def _diff_attn_kernel(q1_ref, q2_ref, k1_ref, k2_ref, v_ref, lq1_ref, lk1_ref, lq2_ref, lk2_ref, sg_ref,
                      o_ref, *, lam_init, qb):
    lam = (jnp.exp(jnp.sum(lq1_ref[...] * lk1_ref[...], axis=-1, keepdims=True))
           - jnp.exp(jnp.sum(lq2_ref[...] * lk2_ref[...], axis=-1, keepdims=True)) + lam_init)
    v = v_ref[0]
    k1 = k1_ref[0]
    k2 = k2_ref[0]

    def branch(q, k):
        s = _nt_dot(q, k)
        m = jnp.max(s, axis=-1, keepdims=True)
        p = jnp.exp(s - m)
        l = jnp.sum(p, axis=-1, keepdims=True)
        return jnp.dot(p.astype(BF16), v, preferred_element_type=F32) / l

    for c in range(q1_ref.shape[1] // qb):
        rows = slice(c * qb, (c + 1) * qb)
        o = branch(q1_ref[0, rows, :], k1) - lam * branch(q2_ref[0, rows, :], k2)
        rms = lax.rsqrt(jnp.mean(o * o, axis=-1, keepdims=True) + LN_EPS)
        o_ref[0, rows, :] = (o * rms * sg_ref[...] * (1.0 - lam_init)).astype(o_ref.dtype)


def _diff_attention(qkv, lam_q1, lam_k1, lam_q2, lam_k2, subln_g, lam_init, d_model, heads):
    b, seq, _ = qkv.shape
    hd = d_model // (2 * heads)
    tq = _tile(seq, 1024)
    k0 = d_model // hd
    v0 = 2 * d_model // (2 * hd)
    kern = functools.partial(_diff_attn_kernel, lam_init=lam_init, qb=_tile(tq, ATTN_Q_SUBBLOCK))
    vec = pl.BlockSpec((1, hd), lambda i, h, qi: (0, 0))
    return pl.pallas_call(
        kern,
        grid=(b, heads, seq // tq),
        in_specs=[pl.BlockSpec((1, tq, hd), lambda i, h, qi: (i, qi, 2 * h)),
                  pl.BlockSpec((1, tq, hd), lambda i, h, qi: (i, qi, 2 * h + 1)),
                  pl.BlockSpec((1, seq, hd), lambda i, h, qi: (i, 0, k0 + 2 * h)),
                  pl.BlockSpec((1, seq, hd), lambda i, h, qi: (i, 0, k0 + 2 * h + 1)),
                  pl.BlockSpec((1, seq, 2 * hd), lambda i, h, qi: (i, 0, v0 + h)),
                  vec, vec, vec, vec,
                  pl.BlockSpec((1, 2 * hd), lambda i, h, qi: (0, 0))],
        out_specs=pl.BlockSpec((1, tq, 2 * hd), lambda i, h, qi: (i, qi, h)),
        out_shape=jax.ShapeDtypeStruct((b, seq, d_model), BF16),
        compiler_params=_cparams(("parallel", "parallel", "arbitrary")),
        name="diff_attention",
    )(qkv, qkv, qkv, qkv, qkv, lam_q1, lam_k1, lam_q2, lam_k2, subln_g)


def _proj_ln_router_kernel(*refs, n_lhs, x_bounds, alpha, n_exp, top_k):
    lhs = refs[:n_lhs]
    w_ref = refs[n_lhs]
    x_refs = refs[n_lhs + 1:n_lhs + 1 + len(x_bounds)]
    rest = refs[n_lhs + 1 + len(x_bounds):]
    g_ref, b_ref, wr_ref, br_ref, tri_ref = rest[:5]
    x1_ref, idx_ref, gate_ref, rank_ref, cnt_ref = rest[5:10]
    run_ref = rest[10]

    @pl.when(pl.program_id(0) == 0)
    def _():
        run_ref[...] = jnp.zeros(run_ref.shape, F32)

    tm = x1_ref.shape[0]
    sub = tri_ref.shape[0]
    run = run_ref[:, 0:1]
    for st in range(tm // sub):
        rows = slice(st * sub, (st + 1) * sub)
        off = 0
        acc = None
        for l in lhs:
            kl = l.shape[1]
            part = jnp.dot(l[rows, :], w_ref[off:off + kl, :], preferred_element_type=F32)
            acc = part if acc is None else acc + part
            off += kl
        x_res = _part_value(pl.program_id(0), x_refs, x_bounds, rows)
        x1 = _layer_norm(alpha * x_res + acc, g_ref[...], b_ref[...])
        x1_ref[rows, :] = x1

        logits = _nt_dot(wr_ref[...], x1.astype(BF16)) + br_ref[...]
        eidx = lax.broadcasted_iota(I32, (n_exp, sub), 0)
        vals, idxs, hots = [], [], []
        for _ in range(top_k):
            m = jnp.max(logits, axis=0, keepdims=True)
            idx = jnp.min(jnp.where(logits == m, eidx, n_exp - 1), axis=0, keepdims=True)
            hot = eidx == idx
            vals.append(m)
            idxs.append(idx)
            hots.append(hot)
            logits = jnp.where(hot, -jnp.inf, logits)
        exps = [jnp.exp(v - vals[0]) for v in vals]
        den = exps[0]
        for e in exps[1:]:
            den = den + e
        gate_ref[:, rows] = jnp.concatenate([e / den for e in exps], axis=0)
        idx_ref[:, rows] = jnp.concatenate(idxs, axis=0)

        hot_all = jnp.concatenate([h.astype(BF16) for h in hots], axis=0)
        before = jnp.dot(hot_all, tri_ref[...], preferred_element_type=F32)
        ranks = []
        for kk in range(top_k):
            hf = hots[kk].astype(F32)
            ranks.append(jnp.sum(hf * (run + before[kk * n_exp:(kk + 1) * n_exp, :]), axis=0, keepdims=True))
            run = run + jnp.sum(hf, axis=1, keepdims=True)
        rank_ref[:, rows] = jnp.concatenate(ranks, axis=0).astype(I32)
    run_ref[...] = jnp.broadcast_to(run, run_ref.shape)
    cnt_ref[...] = run_ref[...]


def _proj_ln_router(lhs_list, w, x_parts, ln_g, ln_b, wr_t, br, tri, alpha, tm):
    n, d = sum(p.shape[0] for p in x_parts), x_parts[0].shape[1]
    n_exp = wr_t.shape[0]
    x_bounds = _part_bounds(x_parts, tm)
    kern = functools.partial(_proj_ln_router_kernel, n_lhs=len(lhs_list), x_bounds=x_bounds, alpha=alpha,
                             n_exp=n_exp, top_k=TOP_K)
    row = lambda i: (i, 0)
    fixed = lambda i: (0, 0)
    in_specs = [pl.BlockSpec((tm, l.shape[1]), row) for l in lhs_list] + [
        pl.BlockSpec(w.shape, fixed, pipeline_mode=pl.Buffered(1))] + _part_specs(x_parts, tm, x_bounds) + [
        pl.BlockSpec((1, d), fixed), pl.BlockSpec((1, d), fixed),
        pl.BlockSpec((n_exp, d), fixed), pl.BlockSpec((n_exp, 1), fixed), pl.BlockSpec(tri.shape, fixed)]
    tok = lambda i: (0, i)
    return pl.pallas_call(
        kern,
        grid=(n // tm,),
        in_specs=in_specs,
        out_specs=[pl.BlockSpec((tm, d), row), pl.BlockSpec((TOP_K, tm), tok), pl.BlockSpec((TOP_K, tm), tok),
                   pl.BlockSpec((TOP_K, tm), tok), pl.BlockSpec((n_exp, LANES), fixed)],
        out_shape=[jax.ShapeDtypeStruct((n, d), F32), jax.ShapeDtypeStruct((TOP_K, n), I32),
                   jax.ShapeDtypeStruct((TOP_K, n), F32), jax.ShapeDtypeStruct((TOP_K, n), I32),
                   jax.ShapeDtypeStruct((n_exp, LANES), F32)],
        scratch_shapes=[pltpu.VMEM((n_exp, LANES), F32)],
        compiler_params=_cparams(("arbitrary",)),
        name="proj_ln_router",
    )(*lhs_list, w, *x_parts, ln_g, ln_b, wr_t, br, tri)


def _dispatch_kernel(dest_ref, x_ref, xs_in_ref, xs_ref, xp_ref, sem, *, top_k):
    del xs_in_ref
    tm, d = x_ref.shape
    x = x_ref[...]
    lo = lax.bitcast_convert_type(x[:, :d // 2].astype(BF16).astype(F32), U32)
    hi = lax.bitcast_convert_type(x[:, d // 2:].astype(BF16).astype(F32), U32)
    xp_ref[...] = jnp.bitwise_or(lax.shift_right_logical(lo, jnp.uint32(16)),
                                 jnp.bitwise_and(hi, jnp.uint32(0xFFFF0000))).reshape(xp_ref.shape)

    for kk in range(top_k):
        def issue(g, carry, kk=kk):
            for r in range(SUBLANES):
                dst = xs_ref.at[pl.ds(dest_ref[kk * tm + g * SUBLANES + r], 1)]
                pltpu.make_async_copy(xp_ref.at[g, pl.ds(r, 1)], dst, sem).start(priority=r % DMA_PRIORITIES)
            return carry

        lax.fori_loop(0, tm // SUBLANES, issue, 0)
    for _ in range(top_k):
        pltpu.make_async_copy(xs_ref.at[pl.ds(0, tm)], xs_ref.at[pl.ds(0, tm)], sem).wait()


def _dispatch(dest_flat, x1, xs_zero, tm):
    n, d = x1.shape
    kern = functools.partial(_dispatch_kernel, top_k=TOP_K)
    return pl.pallas_call(
        kern,
        grid=(n // tm,),
        in_specs=[pl.BlockSpec((TOP_K * tm,), lambda i: (i,), memory_space=pltpu.SMEM),
                  pl.BlockSpec((tm, d), lambda i: (i, 0)),
                  pl.BlockSpec(memory_space=pl.ANY)],
        out_specs=pl.BlockSpec(memory_space=pl.ANY),
        out_shape=jax.ShapeDtypeStruct(xs_zero.shape, U32),
        scratch_shapes=[pltpu.VMEM((tm // SUBLANES, SUBLANES, d // 2), U32), pltpu.SemaphoreType.DMA(())],
        input_output_aliases={2: 0},
        compiler_params=_cparams(("arbitrary",)),
        name="moe_dispatch",
    )(dest_flat, x1, xs_zero)


def _expert_kernel(be_ref, nv_ref, xs_ref, wg_ref, wu_ref, bg_ref, bu_ref, wd_ref, bd_ref, y_ref, xb_ref, *, tn):
    del be_ref
    i = pl.program_id(0)
    half = xs_ref.shape[1]
    d_ff = wg_ref.shape[2]

    @pl.when(i < nv_ref[0])
    def _():
        u = xs_ref[...]
        lo = lax.bitcast_convert_type(lax.shift_left(u, jnp.uint32(16)), F32)
        hi = lax.bitcast_convert_type(jnp.bitwise_and(u, jnp.uint32(0xFFFF0000)), F32)
        xb_ref[:, :half] = lo.astype(BF16)
        xb_ref[:, half:] = hi.astype(BF16)
        xb = xb_ref[...]
        acc = None
        for c in range(d_ff // tn):
            cols = slice(c * tn, (c + 1) * tn)
            hg = jnp.dot(xb, wg_ref[0, :, cols], preferred_element_type=F32) + bg_ref[0, :, cols]
            hu = jnp.dot(xb, wu_ref[0, :, cols], preferred_element_type=F32) + bu_ref[0, :, cols]
            gate = jnp.minimum(hg, SWIGLU_LIMIT)
            up = jnp.clip(hu, -SWIGLU_LIMIT, SWIGLU_LIMIT)
            glu = gate / (1.0 + jnp.exp(-SWIGLU_ALPHA * gate))
            part = jnp.dot(((up + 1.0) * glu).astype(BF16), wd_ref[0, cols, :], preferred_element_type=F32)
            acc = part if acc is None else acc + part
        y_ref[...] = acc + bd_ref[0]

    @pl.when(i >= nv_ref[0])
    def _():
        y_ref[...] = jnp.broadcast_to(bd_ref[0], y_ref.shape)


def _expert_ffn(block_e, n_valid, xs, wg, wu, bg, bu, wd, bd, tm, tn):
    p, half = xs.shape
    d = 2 * half
    d_ff = wg.shape[2]

    def blk(i, nv):
        return jnp.minimum(i, nv[0] - 1)

    def expert(shape):
        return pl.BlockSpec(shape, lambda i, be, nv: (be[blk(i, nv)], 0, 0), pipeline_mode=pl.Buffered(1))

    grid_spec = pltpu.PrefetchScalarGridSpec(
        num_scalar_prefetch=2,
        grid=(p // tm,),
        in_specs=[pl.BlockSpec((tm, half), lambda i, be, nv: (blk(i, nv), 0)),
                  expert((1, d, d_ff)), expert((1, d, d_ff)), expert((1, 1, d_ff)), expert((1, 1, d_ff)),
                  expert((1, d_ff, d)), expert((1, 1, d))],
        out_specs=pl.BlockSpec((tm, d), lambda i, be, nv: (i, 0)),
        scratch_shapes=[pltpu.VMEM((tm, d), BF16)])
    return pl.pallas_call(
        functools.partial(_expert_kernel, tn=tn),
        grid_spec=grid_spec,
        out_shape=jax.ShapeDtypeStruct((p, d), F32),
        compiler_params=_cparams(("arbitrary",)),
        name="moe_experts",
    )(block_e, n_valid, xs, wg, wu, bg, bu, wd, bd)


def _combine_kernel(dest_ref, x_ref, gates_ref, g_ref, b_ref, y_ref, y3_ref, *refs, top_k, alpha, out_bounds):
    o_refs = refs[:len(out_bounds)]
    buf_ref, sem = refs[len(out_bounds):]
    tm, d = x_ref.shape

    for kk in range(top_k):
        def issue(g, carry, kk=kk):
            for r in range(SUBLANES):
                src = y_ref.at[pl.ds(dest_ref[kk * tm + g * SUBLANES + r], 1)]
                pltpu.make_async_copy(src, buf_ref.at[kk, g, pl.ds(r, 1)], sem).start(priority=r % DMA_PRIORITIES)
            return carry

        lax.fori_loop(0, tm // SUBLANES, issue, 0)
    for kk in range(top_k):
        pltpu.make_async_copy(y3_ref.at[pl.ds(0, tm // SUBLANES)], buf_ref.at[kk], sem).wait()
    ff = gates_ref[:, 0:1] * buf_ref[0].reshape(tm, d)
    for kk in range(1, top_k):
        ff = ff + gates_ref[:, kk:kk + 1] * buf_ref[kk].reshape(tm, d)
    res = _layer_norm(alpha * x_ref[...] + ff, g_ref[...], b_ref[...])
    if len(o_refs) == 1:
        o_refs[0][...] = res
    else:
        i = pl.program_id(0)
        for o_ref, (lo, hi) in zip(o_refs, out_bounds):
            @pl.when(jnp.logical_and(i >= lo, i < hi))
            def _(o_ref=o_ref):
                o_ref[...] = res


def _combine(dest_flat, x1, gates_t, ln_g, ln_b, y, alpha, tm, out_rows):
    n, d = x1.shape
    assert sum(out_rows) == n
    outs = [jax.ShapeDtypeStruct((r, d), F32) for r in out_rows]
    out_bounds = _part_bounds(outs, tm)
    kern = functools.partial(_combine_kernel, top_k=TOP_K, alpha=alpha, out_bounds=out_bounds)
    return pl.pallas_call(
        kern,
        grid=(n // tm,),
        in_specs=[pl.BlockSpec((TOP_K * tm,), lambda i: (i,), memory_space=pltpu.SMEM),
                  pl.BlockSpec((tm, d), lambda i: (i, 0)),
                  pl.BlockSpec((tm, TOP_K), lambda i: (i, 0)),
                  pl.BlockSpec((1, d), lambda i: (0, 0)),
                  pl.BlockSpec((1, d), lambda i: (0, 0)),
                  pl.BlockSpec(memory_space=pl.ANY),
                  pl.BlockSpec(memory_space=pl.ANY)],
        out_specs=_part_specs(outs, tm, out_bounds),
        out_shape=outs,
        scratch_shapes=[pltpu.VMEM((TOP_K, tm // SUBLANES, SUBLANES, d), F32), pltpu.SemaphoreType.DMA(())],
        compiler_params=_cparams(("arbitrary",)),
        name="moe_combine",
    )(dest_flat, x1, gates_t, ln_g, ln_b, y, y.reshape(y.shape[0] // SUBLANES, SUBLANES, d))


MOE_ROW_TILE = 512
MOE_FF_TILE = 512
MOE_TOKEN_TILE = 512
ROUTER_TILE = 512
ROUTER_SUBTILE = 256


def _moe_layer(x1, idx, gates, rank, cnt, layer, ew, ln_g, ln_b, alpha, xs_buf, out_rows):
    n, d = x1.shape
    n_exp = cnt.shape[0]
    tm_e = _tile(n * TOP_K, MOE_ROW_TILE)
    tok = _tile(math.gcd(n, *out_rows), MOE_TOKEN_TILE)
    counts = cnt[:, 0].astype(I32)
    padded = ((counts + tm_e - 1) // tm_e) * tm_e
    pend = jnp.cumsum(padded)
    pstart = pend - padded
    dest = rank
    for e in range(n_exp):
        dest = dest + jnp.where(idx == e, pstart[e], 0)
    n_blocks = (n * TOP_K) // tm_e + n_exp
    starts = jnp.arange(n_blocks, dtype=I32) * tm_e
    block_e = jnp.minimum(jnp.sum((starts[:, None] >= pend[None, :]).astype(I32), axis=1), n_exp - 1) + layer * n_exp
    n_valid = (pend[-1:] // tm_e).astype(I32)
    dest_flat = dest.reshape(TOP_K, n // tok, tok).transpose(1, 0, 2).reshape(-1)
    if xs_buf is None:
        xs_buf = jnp.zeros((n_blocks * tm_e, d // 2), U32)
    xs = _dispatch(dest_flat, x1, xs_buf, tok)
    y = _expert_ffn(block_e, n_valid, xs, *ew, tm_e, _tile(ew[0].shape[2], MOE_FF_TILE))
    return _combine(dest_flat, x1, gates.T, ln_g, ln_b, y, alpha, tok, out_rows), xs


def kernel(x_prompt, x_sample, e_w_in, e_w_pool, e_pool_scale, e_w_gate2, e_b_gate2, e_gla_norm_g, e_w_out,
           o_w_qkv, o_lam_q1, o_lam_k1, o_lam_q2, o_lam_k2, o_subln_g, o_w_out,
           ln1_g, ln1_b, ln2_g, ln2_b,
           moe_w_router, moe_b_router, moe_w_gate_up, moe_b_gate_up, moe_w_down, moe_b_down):
    depth = ln1_g.shape[0]
    alpha = (2.0 * depth) ** 0.25
    nb_p, seq, d = x_prompt.shape
    assert x_sample.shape[1:] == (seq, d)
    bsz = nb_p + x_sample.shape[0]
    n = bsz * seq
    x = [x_prompt.reshape(nb_p * seq, d), x_sample.reshape(n - nb_p * seq, d)]
    final_rows = tuple(p.shape[0] for p in x)

    n_exp = moe_w_router.shape[2]
    d_ff = moe_w_down.shape[2]
    w_gate, w_up = _deinterleave_bf16(moe_w_gate_up.reshape(depth * n_exp, d, 2 * d_ff))
    ew = (w_gate, w_up,
          moe_b_gate_up[..., 0::2].reshape(depth * n_exp, 1, d_ff),
          moe_b_gate_up[..., 1::2].reshape(depth * n_exp, 1, d_ff),
          _cast_bf16(moe_w_down.reshape(depth * n_exp, d_ff, d)),
          moe_b_down.reshape(depth * n_exp, 1, d))
    wr_t = jnp.swapaxes(moe_w_router, 1, 2).astype(BF16)
    tm_r = _tile(math.gcd(n, nb_p * seq), ROUTER_TILE)
    sub_r = _tile(tm_r, ROUTER_SUBTILE)
    tri = (jnp.arange(sub_r)[:, None] < jnp.arange(sub_r)[None, :]).astype(BF16)

    pool_width = e_w_pool.shape[1] * e_w_pool.shape[2]
    gla_width = d - pool_width
    keyw = gla_width // 2
    main_w = pool_width + 2 * keyw + 2 * gla_width
    hd = d // (2 * DIFF_HEADS)

    xs_buf = None
    for l in range(depth):
        i = l // 2
        if l % 2 == 0:
            w_in = e_w_in[i]
            h_main = _matmul(x, w_in[:, :main_w].astype(BF16), F32, "in_proj").reshape(bsz, seq, main_w)
            w_gl = jnp.pad(w_in[:, main_w:], ((0, 0), (0, LANES - 2 * GLA_GATE_RANK))).astype(BF16)
            h_gate = _matmul(x, w_gl, F32, "in_proj_gates").reshape(bsz, seq, LANES)
            a_out = _pool_mixer(h_main, e_w_pool[i].astype(BF16), e_pool_scale[i][None, :], pool_width)
            wg2 = e_w_gate2[i].astype(BF16)
            zpad = jnp.zeros((LANES - 2 * GLA_GATE_RANK, keyw), BF16)
            zrank = jnp.zeros((GLA_GATE_RANK, keyw), BF16)
            wgf = jnp.concatenate([wg2[0], zrank, zpad], axis=0)
            wgb = jnp.concatenate([zrank, wg2[1], zpad], axis=0)
            b_out = _gla_mixer(h_main, h_gate, wgf, wgb, e_b_gate2[i][0][None, :], e_b_gate2[i][1][None, :],
                               e_gla_norm_g[i][None, :], pool_width, GLA_HEADS)
            lhs = [a_out.reshape(n, pool_width), b_out.reshape(n, gla_width)]
            w_out = e_w_out[i].astype(BF16)
        else:
            lam_init = 0.8 - 0.6 * math.exp(-0.3 * l)
            cos, sin = _rope_tables(seq, hd)
            x_all = x[0] if len(x) == 1 else jnp.concatenate(x, axis=0)
            qkv = _qkv_proj(x_all, o_w_qkv[i].astype(BF16), cos, sin, seq, d, hd).reshape(bsz, seq, 3 * d)
            attn = _diff_attention(qkv, o_lam_q1[i][None, :], o_lam_k1[i][None, :], o_lam_q2[i][None, :],
                                   o_lam_k2[i][None, :], o_subln_g[i][None, :], lam_init, d, DIFF_HEADS)
            lhs = [attn.reshape(n, d)]
            w_out = o_w_out[i].astype(BF16)
        x1, idx, gates, rank, cnt = _proj_ln_router(
            lhs, w_out, x, ln1_g[l][None, :], ln1_b[l][None, :], wr_t[l], moe_b_router[l][:, None], tri,
            alpha, tm_r)
        out_rows = final_rows if l == depth - 1 else (n,)
        x, xs_buf = _moe_layer(x1, idx, gates, rank, cnt, l, ew, ln2_g[l][None, :], ln2_b[l][None, :], alpha,
                               xs_buf, out_rows)

    return (x[0].reshape(x_prompt.shape), x[1].reshape(x_sample.shape))
```

```python
import functools
import math

import jax
import jax.numpy as jnp
from jax import lax
from jax.experimental import pallas as pl
from jax.experimental.pallas import tpu as pltpu

F32 = jnp.float32
BF16 = jnp.bfloat16
I32 = jnp.int32
U32 = jnp.uint32

POOL_WINDOWS = (2, 4, 8, 16)
GLA_HEADS = 4
GLA_GATE_RANK = 16
GLA_GATE_NORM = 16.0
GLA_CHUNK = 64
DIFF_HEADS = 8
ROPE_THETA = 10000.0
TOP_K = 4
SWIGLU_LIMIT = 7.0
SWIGLU_ALPHA = 1.702
LN_EPS = 1e-5

V7X_VMEM_BYTES = 64 * 2 ** 20
VMEM_LIMIT = V7X_VMEM_BYTES - 8 * 2 ** 20
LANES = 128
SUBLANES = 8
SUBLANE_SHIFT = 3
DMA_PRIORITIES = 2


def _tile(n, pref):
    t = min(n, pref)
    while n % t:
        t //= 2
    return t


def _cparams(sem):
    return pltpu.CompilerParams(dimension_semantics=sem, vmem_limit_bytes=VMEM_LIMIT)


def _nt_dot(a, b):
    return lax.dot_general(a, b, (((1,), (1,)), ((), ())), preferred_element_type=F32)


def _layer_norm(y, g, b):
    mu = jnp.mean(y, axis=-1, keepdims=True)
    yc = y - mu
    var = jnp.mean(yc * yc, axis=-1, keepdims=True)
    return yc * lax.rsqrt(var + LN_EPS) * g + b


def _part_bounds(parts, tm):
    bounds, lo = [], 0
    for p in parts:
        assert p.shape[0] % tm == 0, (p.shape, tm)
        bounds.append((lo, lo + p.shape[0] // tm))
        lo = bounds[-1][1]
    return tuple(bounds)


def _part_specs(parts, tm, bounds):
    def spec(p, lo, hi):
        return pl.BlockSpec((tm, p.shape[1]), lambda i, *_: (jnp.clip(i - lo, 0, hi - lo - 1), 0))
    return [spec(p, lo, hi) for p, (lo, hi) in zip(parts, bounds)]


def _part_value(i, refs, bounds, rows=slice(None)):
    val = refs[-1][rows, :]
    for ref, (_, hi) in zip(reversed(refs[:-1]), reversed(bounds[:-1])):
        val = jnp.where(i < hi, ref[rows, :], val)
    return val


def _mm_kernel(*refs, bounds):
    x_refs = refs[:len(bounds)]
    w_ref, o_ref, xb_ref = refs[len(bounds):]

    @pl.when(pl.program_id(1) == 0)
    def _():
        xb_ref[...] = _part_value(pl.program_id(0), x_refs, bounds).astype(BF16)

    o_ref[...] = jnp.dot(xb_ref[...], w_ref[...], preferred_element_type=F32).astype(o_ref.dtype)


def _matmul(x_parts, w, out_dtype, name, tm_pref=1024, tn_pref=1024):
    m = sum(p.shape[0] for p in x_parts)
    k, n = w.shape
    tm = _tile(math.gcd(*[p.shape[0] for p in x_parts]), tm_pref)
    tn = _tile(n, tn_pref)
    bounds = _part_bounds(x_parts, tm)
    return pl.pallas_call(
        functools.partial(_mm_kernel, bounds=bounds),
        grid=(m // tm, n // tn),
        in_specs=_part_specs(x_parts, tm, bounds) + [pl.BlockSpec((k, tn), lambda i, j: (0, j))],
        out_specs=pl.BlockSpec((tm, tn), lambda i, j: (i, j)),
        out_shape=jax.ShapeDtypeStruct((m, n), out_dtype),
        scratch_shapes=[pltpu.VMEM((tm, k), BF16)],
        compiler_params=_cparams(("parallel", "arbitrary")),
        name=name,
    )(*x_parts, w)


MXU_TILE = 256


def _cast_kernel(x_ref, o_ref):
    o_ref[...] = x_ref[...].astype(o_ref.dtype)


def _cast_bf16(x):
    a, r, c = x.shape
    tr = _tile(r, 512)
    return pl.pallas_call(
        _cast_kernel,
        grid=(a, r // tr),
        in_specs=[pl.BlockSpec((1, tr, c), lambda i, j: (i, j, 0))],
        out_specs=pl.BlockSpec((1, tr, c), lambda i, j: (i, j, 0)),
        out_shape=jax.ShapeDtypeStruct(x.shape, BF16),
        compiler_params=_cparams(("parallel", "parallel")),
        name="cast_bf16",
    )(x)


def _deinterleave_kernel(w_ref, p_ref, g_ref, u_ref):
    w = w_ref[0].astype(BF16)
    hm = MXU_TILE // 2
    for c in range(w.shape[1] // MXU_TILE):
        r = jnp.dot(w[:, c * MXU_TILE:(c + 1) * MXU_TILE], p_ref[...], preferred_element_type=F32).astype(BF16)
        g_ref[0, :, c * hm:(c + 1) * hm] = r[:, :hm]
        u_ref[0, :, c * hm:(c + 1) * hm] = r[:, hm:]


def _deinterleave_bf16(w):
    a, r, c2 = w.shape
    tr, tc2 = _tile(r, 1024), _tile(c2, 1024)
    assert tc2 % MXU_TILE == 0
    rows = jnp.arange(MXU_TILE)
    perm = (jnp.arange(MXU_TILE)[None, :] == (rows // 2 + (rows % 2) * (MXU_TILE // 2))[:, None]).astype(BF16)
    out = jax.ShapeDtypeStruct((a, r, c2 // 2), BF16)
    return pl.pallas_call(
        _deinterleave_kernel,
        grid=(a, r // tr, c2 // tc2),
        in_specs=[pl.BlockSpec((1, tr, tc2), lambda i, j, k: (i, j, k)),
                  pl.BlockSpec((MXU_TILE, MXU_TILE), lambda i, j, k: (0, 0))],
        out_specs=[pl.BlockSpec((1, tr, tc2 // 2), lambda i, j, k: (i, j, k))] * 2,
        out_shape=[out, out],
        compiler_params=_cparams(("parallel", "parallel", "parallel")),
        name="deinterleave_bf16",
    )(w, perm)


def _rope_table_kernel(cos_ref, sin_ref, *, half, theta):
    rows, hd = cos_ref.shape
    pos = (lax.broadcasted_iota(I32, (rows, hd), 0) + pl.program_id(0) * rows).astype(F32)
    lane = lax.broadcasted_iota(I32, (rows, hd), 1)
    fi = jnp.where(lane >= half, lane - half, lane).astype(F32)
    inv = jnp.exp(fi * (-math.log(theta) / half))
    ang = pos * inv
    cos_ref[...] = jnp.cos(ang)
    s = jnp.sin(ang)
    sin_ref[...] = jnp.where(lane < half, -s, s)


def _rope_tables(seq, hd):
    rows = _tile(seq, 256)
    return pl.pallas_call(
        functools.partial(_rope_table_kernel, half=hd // 2, theta=ROPE_THETA),
        grid=(seq // rows,),
        out_specs=[pl.BlockSpec((rows, hd), lambda i: (i, 0))] * 2,
        out_shape=[jax.ShapeDtypeStruct((seq, hd), F32)] * 2,
        compiler_params=_cparams(("parallel",)),
        name="rope_tables",
    )()


def _qkv_kernel(x_ref, w_ref, cos_ref, sin_ref, o_ref, xb_ref, *, n_q_tiles, n_rope_tiles, hd, q_scale):
    j = pl.program_id(1)

    @pl.when(j == 0)
    def _():
        xb_ref[...] = x_ref[...].astype(BF16)

    acc = jnp.dot(xb_ref[...], w_ref[...], preferred_element_type=F32)
    tn = acc.shape[1]

    rot = jnp.where(j < n_rope_tiles, jnp.where(j < n_q_tiles, q_scale, 1.0), 0.0)
    cos = cos_ref[...] * rot + jnp.where(j < n_rope_tiles, 0.0, 1.0)
    sin = sin_ref[...] * rot
    for h in range(tn // hd):
        t = acc[:, h * hd:(h + 1) * hd]
        r = t * cos + pltpu.roll(t, hd // 2, axis=1) * sin
        o_ref[:, h * hd:(h + 1) * hd] = r.astype(o_ref.dtype)


def _qkv_proj(x, w, cos, sin, seq, d_model, hd):
    m, k = x.shape
    n = w.shape[1]
    tm, tn = _tile(seq, 1024), _tile(d_model, 1024)
    spb = seq // tm
    kern = functools.partial(_qkv_kernel, n_q_tiles=d_model // tn, n_rope_tiles=2 * d_model // tn,
                             hd=hd, q_scale=hd ** -0.5)
    return pl.pallas_call(
        kern,
        grid=(m // tm, n // tn),
        in_specs=[pl.BlockSpec((tm, k), lambda i, j: (i, 0)),
                  pl.BlockSpec((k, tn), lambda i, j: (0, j)),
                  pl.BlockSpec((tm, hd), lambda i, j: (i % spb, 0)),
                  pl.BlockSpec((tm, hd), lambda i, j: (i % spb, 0))],
        out_specs=pl.BlockSpec((tm, tn), lambda i, j: (i, j)),
        out_shape=jax.ShapeDtypeStruct((m, n), BF16),
        scratch_shapes=[pltpu.VMEM((tm, k), BF16)],
        compiler_params=_cparams(("parallel", "arbitrary")),
        name="qkv_rope",
    )(x, w, cos, sin)


POOL_HALO = 8


def _pool_kernel(u_ref, w_ref, sc_ref, o_ref, pad_ref, *, windows, gw, seq, rc):
    assert max(windows) // 2 <= POOL_HALO
    pad_ref[0:POOL_HALO, :] = jnp.zeros((POOL_HALO, gw), F32)
    pad_ref[POOL_HALO + seq:POOL_HALO + seq + POOL_HALO, :] = jnp.zeros((POOL_HALO, gw), F32)
    for g, w in enumerate(windows):
        hw = w // 2
        pad_ref[POOL_HALO:POOL_HALO + seq, :] = u_ref[0, :, g * gw:(g + 1) * gw]
        for c in range(seq // rc):
            base = POOL_HALO + c * rc
            acc = pad_ref[base - hw:base - hw + rc, :]
            for off in range(-hw + 1, hw):
                acc = acc + pad_ref[base + off:base + off + rc, :]
            pos = lax.broadcasted_iota(I32, (rc, gw), 0) + c * rc
            cnt = (jnp.minimum(pos + hw, seq) - jnp.maximum(pos - hw, 0)).astype(F32)
            pooled = acc / cnt - pad_ref[base:base + rc, :]
            mixed = jnp.dot(pooled.astype(BF16), w_ref[g], preferred_element_type=F32)
            mixed = mixed * sc_ref[:, g * gw:(g + 1) * gw]
            o_ref[0, c * rc:(c + 1) * rc, g * gw:(g + 1) * gw] = mixed.astype(o_ref.dtype)


def _pool_mixer(h_main, w_pool, pool_scale, pool_width):
    b, seq, _ = h_main.shape
    g = w_pool.shape[0]
    gw = pool_width // g
    rc = _tile(seq, 256)
    kern = functools.partial(_pool_kernel, windows=POOL_WINDOWS, gw=gw, seq=seq, rc=rc)
    return pl.pallas_call(
        kern,
        grid=(b,),
        in_specs=[pl.BlockSpec((1, seq, pool_width), lambda i: (i, 0, 0)),
                  pl.BlockSpec((g, gw, gw), lambda i: (0, 0, 0)),
                  pl.BlockSpec((1, pool_width), lambda i: (0, 0))],
        out_specs=pl.BlockSpec((1, seq, pool_width), lambda i: (i, 0, 0)),
        out_shape=jax.ShapeDtypeStruct((b, seq, pool_width), BF16),
        scratch_shapes=[pltpu.VMEM((seq + 2 * POOL_HALO, gw), F32)],
        compiler_params=_cparams(("parallel",)),
        name="pool_mixer",
    )(h_main, w_pool, pool_scale)


def _log_sigmoid(x):
    return jnp.minimum(x, 0.0) - jnp.log(1.0 + jnp.exp(-jnp.abs(x)))


def _gla_kernel(q_ref, k_ref, v_ref, r_ref, gl_ref, wgf_ref, wgb_ref, bgf_ref, bgb_ref, gn_ref, o_ref,
                qm_f, km_f, kd_f, qx_f, kx_f, qm_b, km_b, kd_b, qx_b, kx_b,
                vt_ref, dec_f, dec_b, of_ref, ob_ref, sf_ref, sb_ref, *, seq, dk, dv, chunk, rb):
    pair = 2 * chunk
    n_pairs = seq // pair
    q_scale = dk ** -0.5

    def prep(blk, carry):
        r0 = pl.multiple_of(blk * rb, rb)
        rows = pl.ds(r0, rb)
        q = q_ref[0, rows, :] * q_scale
        k = k_ref[0, rows, :]
        gl = gl_ref[0, rows, :].astype(BF16)
        row = lax.broadcasted_iota(I32, (rb, dk), 0)
        rc = jnp.bitwise_and(row, chunk - 1)
        second = jnp.bitwise_and(row, chunk) != 0

        def scans(pre):
            g = _log_sigmoid(pre) / GLA_GATE_NORM
            pfx = g
            sfx = g
            sh = 1
            while sh < chunk:
                pfx = pfx + jnp.where(rc >= sh, pltpu.roll(pfx, sh, axis=0), 0.0)
                sfx = sfx + jnp.where(rc + sh < chunk, pltpu.roll(sfx, rb - sh, axis=0), 0.0)
                sh *= 2
            tot = pfx + sfx - g
            return pfx, sfx, tot

        def emit(bcum, tot, q_extra_rows, qm, km, kd, qx, kx, dec):
            tot_prev = pltpu.roll(tot, chunk, axis=0)
            tot_next = pltpu.roll(tot, rb - chunk, axis=0)
            other = jnp.where(second, tot_prev, tot_next)
            qm[rows, :] = (q * jnp.exp(bcum)).astype(BF16)
            km[rows, :] = (k * jnp.exp(-bcum)).astype(BF16)
            kd[rows, :] = (k * jnp.exp(tot - bcum)).astype(BF16)
            qx[rows, :] = (q * jnp.exp(bcum + jnp.where(q_extra_rows, other, 0.0))).astype(BF16)
            kx[rows, :] = (k * jnp.exp(tot - bcum + jnp.where(q_extra_rows, 0.0, other))).astype(BF16)
            d = jnp.exp(tot + other)
            for p in range(rb // pair):
                dec[blk * (rb // pair) + p] = d[p * pair:p * pair + 1, :]

        pre_f = jnp.dot(gl, wgf_ref[...], preferred_element_type=F32) + bgf_ref[...]
        pfx, _, tot = scans(pre_f)
        emit(pfx, tot, second, qm_f, km_f, kd_f, qx_f, kx_f, dec_f)
        pre_b = jnp.dot(gl, wgb_ref[...], preferred_element_type=F32) + bgb_ref[...]
        _, sfx, tot = scans(pre_b)
        emit(sfx, tot, jnp.logical_not(second), qm_b, km_b, kd_b, qx_b, kx_b, dec_b)
        return carry

    lax.fori_loop(0, seq // rb, prep, 0)

    for p in range(n_pairs):
        vt_ref[p] = v_ref[0, p * pair:(p + 1) * pair, :].T.astype(BF16)

    ri = lax.broadcasted_iota(I32, (pair, pair), 0)
    ci = lax.broadcasted_iota(I32, (pair, pair), 1)
    same = (ri >= chunk) == (ci >= chunk)
    m1_f = jnp.logical_and(same, ci <= ri)
    m2_f = jnp.logical_and(ri >= chunk, ci < chunk)
    m1_b = jnp.logical_and(same, ci >= ri)
    m2_b = jnp.logical_and(ri < chunk, ci >= chunk)

    sf_ref[...] = jnp.zeros((dv, dk), F32)
    sb_ref[...] = jnp.zeros((dv, dk), F32)

    def step(p, qm, km, kd, qx, kx, dec, s_ref, out, m1, m2):
        rows = pl.ds(pl.multiple_of(p * pair, pair), pair)
        a1 = _nt_dot(qm[rows, :], km[rows, :])
        a2 = _nt_dot(qm[rows, :], kd[rows, :])
        att = jnp.where(m1, a1, jnp.where(m2, a2, 0.0)).astype(BF16)
        vb = v_ref[0, rows, :].astype(BF16)
        st = s_ref[...]
        o = jnp.dot(att, vb, preferred_element_type=F32) + _nt_dot(qx[rows, :], st.astype(BF16))
        out[rows, :] = o
        s_ref[...] = st * dec[p] + jnp.dot(vt_ref[p], kx[rows, :], preferred_element_type=F32)

    def scan(i, carry):
        step(i, qm_f, km_f, kd_f, qx_f, kx_f, dec_f, sf_ref, of_ref, m1_f, m2_f)
        step(n_pairs - 1 - i, qm_b, km_b, kd_b, qx_b, kx_b, dec_b, sb_ref, ob_ref, m1_b, m2_b)
        return carry

    lax.fori_loop(0, n_pairs, scan, 0, unroll=4)

    def finish(blk, carry):
        rows = pl.ds(pl.multiple_of(blk * rb, rb), rb)
        o = of_ref[rows, :] + ob_ref[rows, :]
        mu = jnp.mean(o, axis=-1, keepdims=True)
        oc = o - mu
        var = jnp.mean(oc * oc, axis=-1, keepdims=True)
        on = oc * lax.rsqrt(var + LN_EPS) * gn_ref[...]
        r = r_ref[0, rows, :]
        silu = r / (1.0 + jnp.exp(-r))
        o_ref[0, rows, :] = (silu * on).astype(o_ref.dtype)
        return carry

    lax.fori_loop(0, seq // rb, finish, 0)


def _gla_mixer(h_main, h_gate, wgf, wgb, bgf, bgb, gla_norm_g, pool_width, heads):
    b, seq, _ = h_main.shape
    gla_width = gla_norm_g.shape[1]
    dv = gla_width // heads
    dk = dv // 2
    keyw = heads * dk
    chunk = GLA_CHUNK
    pair = 2 * chunk
    assert seq % pair == 0 and chunk & (chunk - 1) == 0
    rb = _tile(seq, 256)
    assert rb % pair == 0
    q0, k0 = pool_width // dk, (pool_width + keyw) // dk
    v0, r0 = (pool_width + 2 * keyw) // dv, (pool_width + 2 * keyw + gla_width) // dv
    assert pool_width % dk == 0 and (pool_width + 2 * keyw) % dv == 0
    gk = h_gate.shape[2]
    kern = functools.partial(_gla_kernel, seq=seq, dk=dk, dv=dv, chunk=chunk, rb=rb)
    row_dk = pltpu.VMEM((seq, dk), BF16)
    return pl.pallas_call(
        kern,
        grid=(b, heads),
        in_specs=[pl.BlockSpec((1, seq, dk), lambda i, h: (i, 0, q0 + h)),
                  pl.BlockSpec((1, seq, dk), lambda i, h: (i, 0, k0 + h)),
                  pl.BlockSpec((1, seq, dv), lambda i, h: (i, 0, v0 + h)),
                  pl.BlockSpec((1, seq, dv), lambda i, h: (i, 0, r0 + h)),
                  pl.BlockSpec((1, seq, gk), lambda i, h: (i, 0, 0)),
                  pl.BlockSpec((gk, dk), lambda i, h: (0, h)),
                  pl.BlockSpec((gk, dk), lambda i, h: (0, h)),
                  pl.BlockSpec((1, dk), lambda i, h: (0, h)),
                  pl.BlockSpec((1, dk), lambda i, h: (0, h)),
                  pl.BlockSpec((1, dv), lambda i, h: (0, h))],
        out_specs=pl.BlockSpec((1, seq, dv), lambda i, h: (i, 0, h)),
        out_shape=jax.ShapeDtypeStruct((b, seq, gla_width), BF16),
        scratch_shapes=[row_dk] * 10 + [
            pltpu.VMEM((seq // pair, dv, pair), BF16),
            pltpu.VMEM((seq // pair, 1, dk), F32),
            pltpu.VMEM((seq // pair, 1, dk), F32),
            pltpu.VMEM((seq, dv), F32),
            pltpu.VMEM((seq, dv), F32),
            pltpu.VMEM((dv, dk), F32),
            pltpu.VMEM((dv, dk), F32)],
        compiler_params=_cparams(("parallel", "arbitrary")),
        name="gla_mixer",
    )(h_main, h_main, h_main, h_main, h_gate, wgf, wgb, bgf, bgb, gla_norm_g)


ATTN_Q_SUBBLOCK = 128


def _diff_attn_kernel(q1_ref, q2_ref, k1_ref, k2_ref, v_ref, lq1_ref, lk1_ref, lq2_ref, lk2_ref, sg_ref,
                      o_ref, *, lam_init, qb):
    lam = (jnp.exp(jnp.sum(lq1_ref[...] * lk1_ref[...], axis=-1, keepdims=True))
           - jnp.exp(jnp.sum(lq2_ref[...] * lk2_ref[...], axis=-1, keepdims=True)) + lam_init)
    v = v_ref[0]
    k1 = k1_ref[0]
    k2 = k2_ref[0]

    def branch(q, k):
        s = _nt_dot(q, k)
        m = jnp.max(s, axis=-1, keepdims=True)
        p = jnp.exp(s - m)
        l = jnp.sum(p, axis=-1, keepdims=True)
        return jnp.dot(p.astype(BF16), v, preferred_element_type=F32) / l

    for c in range(q1_ref.shape[1] // qb):
        rows = slice(c * qb, (c + 1) * qb)
        o = branch(q1_ref[0, rows, :], k1) - lam * branch(q2_ref[0, rows, :], k2)
        rms = lax.rsqrt(jnp.mean(o * o, axis=-1, keepdims=True) + LN_EPS)
        o_ref[0, rows, :] = (o * rms * sg_ref[...] * (1.0 - lam_init)).astype(o_ref.dtype)


def _diff_attention(qkv, lam_q1, lam_k1, lam_q2, lam_k2, subln_g, lam_init, d_model, heads):
    b, seq, _ = qkv.shape
    hd = d_model // (2 * heads)
    tq = _tile(seq, 2048)
    k0 = d_model // hd
    v0 = 2 * d_model // (2 * hd)
    kern = functools.partial(_diff_attn_kernel, lam_init=lam_init, qb=_tile(tq, ATTN_Q_SUBBLOCK))
    vec = pl.BlockSpec((1, hd), lambda i, h, qi: (0, 0))
    return pl.pallas_call(
        kern,
        grid=(b, heads, seq // tq),
        in_specs=[pl.BlockSpec((1, tq, hd), lambda i, h, qi: (i, qi, 2 * h)),
                  pl.BlockSpec((1, tq, hd), lambda i, h, qi: (i, qi, 2 * h + 1)),
                  pl.BlockSpec((1, seq, hd), lambda i, h, qi: (i, 0, k0 + 2 * h)),
                  pl.BlockSpec((1, seq, hd), lambda i, h, qi: (i, 0, k0 + 2 * h + 1)),
                  pl.BlockSpec((1, seq, 2 * hd), lambda i, h, qi: (i, 0, v0 + h)),
                  vec, vec, vec, vec,
                  pl.BlockSpec((1, 2 * hd), lambda i, h, qi: (0, 0))],
        out_specs=pl.BlockSpec((1, tq, 2 * hd), lambda i, h, qi: (i, qi, h)),
        out_shape=jax.ShapeDtypeStruct((b, seq, d_model), BF16),
        compiler_params=_cparams(("parallel", "parallel", "arbitrary")),
        name="diff_attention",
    )(qkv, qkv, qkv, qkv, qkv, lam_q1, lam_k1, lam_q2, lam_k2, subln_g)


def _proj_ln_router_kernel(*refs, n_lhs, x_bounds, alpha, n_exp, top_k):
    lhs = refs[:n_lhs]
    w_ref = refs[n_lhs]
    x_refs = refs[n_lhs + 1:n_lhs + 1 + len(x_bounds)]
    rest = refs[n_lhs + 1 + len(x_bounds):]
    g_ref, b_ref, wr_ref, br_ref, tri_ref = rest[:5]
    x1_ref, idx_ref, gate_ref, rank_ref, cnt_ref = rest[5:10]
    run_ref = rest[10]

    @pl.when(pl.program_id(0) == 0)
    def _():
        run_ref[...] = jnp.zeros(run_ref.shape, F32)

    tm = x1_ref.shape[0]
    sub = tri_ref.shape[0]
    run = run_ref[:, 0:1]
    for st in range(tm // sub):
        rows = slice(st * sub, (st + 1) * sub)
        off = 0
        acc = None
        for l in lhs:
            kl = l.shape[1]
            part = jnp.dot(l[rows, :], w_ref[off:off + kl, :], preferred_element_type=F32)
            acc = part if acc is None else acc + part
            off += kl
        x_res = _part_value(pl.program_id(0), x_refs, x_bounds, rows)
        x1 = _layer_norm(alpha * x_res + acc, g_ref[...], b_ref[...])
        x1_ref[rows, :] = x1

        logits = _nt_dot(wr_ref[...], x1.astype(BF16)) + br_ref[...]
        eidx = lax.broadcasted_iota(I32, (n_exp, sub), 0)
        vals, idxs, hots = [], [], []
        for _ in range(top_k):
            m = jnp.max(logits, axis=0, keepdims=True)
            idx = jnp.min(jnp.where(logits == m, eidx, n_exp - 1), axis=0, keepdims=True)
            hot = eidx == idx
            vals.append(m)
            idxs.append(idx)
            hots.append(hot)
            logits = jnp.where(hot, -jnp.inf, logits)
        exps = [jnp.exp(v - vals[0]) for v in vals]
        den = exps[0]
        for e in exps[1:]:
            den = den + e
        gate_ref[:, rows] = jnp.concatenate([e / den for e in exps], axis=0)
        idx_ref[:, rows] = jnp.concatenate(idxs, axis=0)

        hot_all = jnp.concatenate([h.astype(BF16) for h in hots], axis=0)
        before = jnp.dot(hot_all, tri_ref[...], preferred_element_type=F32)
        ranks = []
        for kk in range(top_k):
            hf = hots[kk].astype(F32)
            ranks.append(jnp.sum(hf * (run + before[kk * n_exp:(kk + 1) * n_exp, :]), axis=0, keepdims=True))
            run = run + jnp.sum(hf, axis=1, keepdims=True)
        rank_ref[:, rows] = jnp.concatenate(ranks, axis=0).astype(I32)
    run_ref[...] = jnp.broadcast_to(run, run_ref.shape)
    cnt_ref[...] = run_ref[...]


def _proj_ln_router(lhs_list, w, x_parts, ln_g, ln_b, wr_t, br, tri, alpha, tm):
    n, d = sum(p.shape[0] for p in x_parts), x_parts[0].shape[1]
    n_exp = wr_t.shape[0]
    x_bounds = _part_bounds(x_parts, tm)
    kern = functools.partial(_proj_ln_router_kernel, n_lhs=len(lhs_list), x_bounds=x_bounds, alpha=alpha,
                             n_exp=n_exp, top_k=TOP_K)
    row = lambda i: (i, 0)
    fixed = lambda i: (0, 0)
    in_specs = [pl.BlockSpec((tm, l.shape[1]), row) for l in lhs_list] + [
        pl.BlockSpec(w.shape, fixed, pipeline_mode=pl.Buffered(1))] + _part_specs(x_parts, tm, x_bounds) + [
        pl.BlockSpec((1, d), fixed), pl.BlockSpec((1, d), fixed),
        pl.BlockSpec((n_exp, d), fixed), pl.BlockSpec((n_exp, 1), fixed), pl.BlockSpec(tri.shape, fixed)]
    tok = lambda i: (0, i)
    return pl.pallas_call(
        kern,
        grid=(n // tm,),
        in_specs=in_specs,
        out_specs=[pl.BlockSpec((tm, d), row), pl.BlockSpec((TOP_K, tm), tok), pl.BlockSpec((TOP_K, tm), tok),
                   pl.BlockSpec((TOP_K, tm), tok), pl.BlockSpec((n_exp, LANES), fixed)],
        out_shape=[jax.ShapeDtypeStruct((n, d), F32), jax.ShapeDtypeStruct((TOP_K, n), I32),
                   jax.ShapeDtypeStruct((TOP_K, n), F32), jax.ShapeDtypeStruct((TOP_K, n), I32),
                   jax.ShapeDtypeStruct((n_exp, LANES), F32)],
        scratch_shapes=[pltpu.VMEM((n_exp, LANES), F32)],
        compiler_params=_cparams(("arbitrary",)),
        name="proj_ln_router",
    )(*lhs_list, w, *x_parts, ln_g, ln_b, wr_t, br, tri)


def _dispatch_kernel(dest_ref, x_ref, xs_in_ref, xs_ref, xp_ref, sem, *, top_k):
    del xs_in_ref
    tm, d = x_ref.shape
    x = x_ref[...]
    lo = lax.bitcast_convert_type(x[:, :d // 2].astype(BF16).astype(F32), U32)
    hi = lax.bitcast_convert_type(x[:, d // 2:].astype(BF16).astype(F32), U32)
    xp_ref[...] = jnp.bitwise_or(lax.shift_right_logical(lo, jnp.uint32(16)),
                                 jnp.bitwise_and(hi, jnp.uint32(0xFFFF0000))).reshape(xp_ref.shape)

    for kk in range(top_k):
        def issue(g, carry, kk=kk):
            for r in range(SUBLANES):
                dst = xs_ref.at[pl.ds(dest_ref[kk * tm + g * SUBLANES + r], 1)]
                pltpu.make_async_copy(xp_ref.at[g, pl.ds(r, 1)], dst, sem).start(priority=r % DMA_PRIORITIES)
            return carry

        lax.fori_loop(0, tm // SUBLANES, issue, 0)
    for _ in range(top_k):
        pltpu.make_async_copy(xs_ref.at[pl.ds(0, tm)], xs_ref.at[pl.ds(0, tm)], sem).wait()


def _dispatch(dest_flat, x1, xs_zero, tm):
    n, d = x1.shape
    kern = functools.partial(_dispatch_kernel, top_k=TOP_K)
    return pl.pallas_call(
        kern,
        grid=(n // tm,),
        in_specs=[pl.BlockSpec((TOP_K * tm,), lambda i: (i,), memory_space=pltpu.SMEM),
                  pl.BlockSpec((tm, d), lambda i: (i, 0)),
                  pl.BlockSpec(memory_space=pl.ANY)],
        out_specs=pl.BlockSpec(memory_space=pl.ANY),
        out_shape=jax.ShapeDtypeStruct(xs_zero.shape, U32),
        scratch_shapes=[pltpu.VMEM((tm // SUBLANES, SUBLANES, d // 2), U32), pltpu.SemaphoreType.DMA(())],
        input_output_aliases={2: 0},
        compiler_params=_cparams(("arbitrary",)),
        name="moe_dispatch",
    )(dest_flat, x1, xs_zero)


def _expert_kernel(be_ref, nv_ref, xs_ref, wg_ref, wu_ref, bg_ref, bu_ref, wd_ref, bd_ref, y_ref, xb_ref, *, tn):
    del be_ref
    i = pl.program_id(0)
    half = xs_ref.shape[1]
    d_ff = wg_ref.shape[2]

    @pl.when(i < nv_ref[0])
    def _():
        u = xs_ref[...]
        lo = lax.bitcast_convert_type(lax.shift_left(u, jnp.uint32(16)), F32)
        hi = lax.bitcast_convert_type(jnp.bitwise_and(u, jnp.uint32(0xFFFF0000)), F32)
        xb_ref[:, :half] = lo.astype(BF16)
        xb_ref[:, half:] = hi.astype(BF16)
        xb = xb_ref[...]
        acc = None
        for c in range(d_ff // tn):
            cols = slice(c * tn, (c + 1) * tn)
            hg = jnp.dot(xb, wg_ref[0, :, cols], preferred_element_type=F32) + bg_ref[0, :, cols]
            hu = jnp.dot(xb, wu_ref[0, :, cols], preferred_element_type=F32) + bu_ref[0, :, cols]
            gate = jnp.minimum(hg, SWIGLU_LIMIT)
            up = jnp.clip(hu, -SWIGLU_LIMIT, SWIGLU_LIMIT)
            glu = gate / (1.0 + jnp.exp(-SWIGLU_ALPHA * gate))
            part = jnp.dot(((up + 1.0) * glu).astype(BF16), wd_ref[0, cols, :], preferred_element_type=F32)
            acc = part if acc is None else acc + part
        y_ref[...] = acc + bd_ref[0]

    @pl.when(i >= nv_ref[0])
    def _():
        y_ref[...] = jnp.broadcast_to(bd_ref[0], y_ref.shape)


def _expert_ffn(block_e, n_valid, xs, wg, wu, bg, bu, wd, bd, tm, tn):
    p, half = xs.shape
    d = 2 * half
    d_ff = wg.shape[2]

    def blk(i, nv):
        return jnp.minimum(i, nv[0] - 1)

    def expert(shape):
        return pl.BlockSpec(shape, lambda i, be, nv: (be[blk(i, nv)], 0, 0), pipeline_mode=pl.Buffered(1))

    grid_spec = pltpu.PrefetchScalarGridSpec(
        num_scalar_prefetch=2,
        grid=(p // tm,),
        in_specs=[pl.BlockSpec((tm, half), lambda i, be, nv: (blk(i, nv), 0)),
                  expert((1, d, d_ff)), expert((1, d, d_ff)), expert((1, 1, d_ff)), expert((1, 1, d_ff)),
                  expert((1, d_ff, d)), expert((1, 1, d))],
        out_specs=pl.BlockSpec((tm, d), lambda i, be, nv: (i, 0)),
        scratch_shapes=[pltpu.VMEM((tm, d), BF16)])
    return pl.pallas_call(
        functools.partial(_expert_kernel, tn=tn),
        grid_spec=grid_spec,
        out_shape=jax.ShapeDtypeStruct((p, d), F32),
        compiler_params=_cparams(("arbitrary",)),
        name="moe_experts",
    )(block_e, n_valid, xs, wg, wu, bg, bu, wd, bd)


def _combine_kernel(dest_ref, x_ref, gates_ref, g_ref, b_ref, y_ref, y3_ref, *refs, top_k, alpha, out_bounds):
    o_refs = refs[:len(out_bounds)]
    buf_ref, sem = refs[len(out_bounds):]
    tm, d = x_ref.shape

    for kk in range(top_k):
        def issue(g, carry, kk=kk):
            for r in range(SUBLANES):
                src = y_ref.at[pl.ds(dest_ref[kk * tm + g * SUBLANES + r], 1)]
                pltpu.make_async_copy(src, buf_ref.at[kk, g, pl.ds(r, 1)], sem).start(priority=r % DMA_PRIORITIES)
            return carry

        lax.fori_loop(0, tm // SUBLANES, issue, 0)
    for kk in range(top_k):
        pltpu.make_async_copy(y3_ref.at[pl.ds(0, tm // SUBLANES)], buf_ref.at[kk], sem).wait()
    ff = gates_ref[:, 0:1] * buf_ref[0].reshape(tm, d)
    for kk in range(1, top_k):
        ff = ff + gates_ref[:, kk:kk + 1] * buf_ref[kk].reshape(tm, d)
    res = _layer_norm(alpha * x_ref[...] + ff, g_ref[...], b_ref[...])
    if len(o_refs) == 1:
        o_refs[0][...] = res
    else:
        i = pl.program_id(0)
        for o_ref, (lo, hi) in zip(o_refs, out_bounds):
            @pl.when(jnp.logical_and(i >= lo, i < hi))
            def _(o_ref=o_ref):
                o_ref[...] = res


def _combine(dest_flat, x1, gates_t, ln_g, ln_b, y, alpha, tm, out_rows):
    n, d = x1.shape
    assert sum(out_rows) == n
    outs = [jax.ShapeDtypeStruct((r, d), F32) for r in out_rows]
    out_bounds = _part_bounds(outs, tm)
    kern = functools.partial(_combine_kernel, top_k=TOP_K, alpha=alpha, out_bounds=out_bounds)
    return pl.pallas_call(
        kern,
        grid=(n // tm,),
        in_specs=[pl.BlockSpec((TOP_K * tm,), lambda i: (i,), memory_space=pltpu.SMEM),
                  pl.BlockSpec((tm, d), lambda i: (i, 0)),
                  pl.BlockSpec((tm, TOP_K), lambda i: (i, 0)),
                  pl.BlockSpec((1, d), lambda i: (0, 0)),
                  pl.BlockSpec((1, d), lambda i: (0, 0)),
                  pl.BlockSpec(memory_space=pl.ANY),
                  pl.BlockSpec(memory_space=pl.ANY)],
        out_specs=_part_specs(outs, tm, out_bounds),
        out_shape=outs,
        scratch_shapes=[pltpu.VMEM((TOP_K, tm // SUBLANES, SUBLANES, d), F32), pltpu.SemaphoreType.DMA(())],
        compiler_params=_cparams(("arbitrary",)),
        name="moe_combine",
    )(dest_flat, x1, gates_t, ln_g, ln_b, y, y.reshape(y.shape[0] // SUBLANES, SUBLANES, d))


MOE_ROW_TILE = 512
MOE_FF_TILE = 512
MOE_TOKEN_TILE = 512
ROUTER_TILE = 512
ROUTER_SUBTILE = 256


def _moe_layer(x1, idx, gates, rank, cnt, layer, ew, ln_g, ln_b, alpha, xs_buf, out_rows):
    n, d = x1.shape
    n_exp = cnt.shape[0]
    tm_e = _tile(n * TOP_K, MOE_ROW_TILE)
    tok = _tile(math.gcd(n, *out_rows), MOE_TOKEN_TILE)
    counts = cnt[:, 0].astype(I32)
    padded = ((counts + tm_e - 1) // tm_e) * tm_e
    pend = jnp.cumsum(padded)
    pstart = pend - padded
    dest = rank
    for e in range(n_exp):
        dest = dest + jnp.where(idx == e, pstart[e], 0)
    n_blocks = (n * TOP_K) // tm_e + n_exp
    starts = jnp.arange(n_blocks, dtype=I32) * tm_e
    block_e = jnp.minimum(jnp.sum((starts[:, None] >= pend[None, :]).astype(I32), axis=1), n_exp - 1) + layer * n_exp
    n_valid = (pend[-1:] // tm_e).astype(I32)
    dest_flat = dest.reshape(TOP_K, n // tok, tok).transpose(1, 0, 2).reshape(-1)
    if xs_buf is None:
        xs_buf = jnp.zeros((n_blocks * tm_e, d // 2), U32)
    xs = _dispatch(dest_flat, x1, xs_buf, tok)
    y = _expert_ffn(block_e, n_valid, xs, *ew, tm_e, _tile(ew[0].shape[2], MOE_FF_TILE))
    return _combine(dest_flat, x1, gates.T, ln_g, ln_b, y, alpha, tok, out_rows), xs


def kernel(x_prompt, x_sample, e_w_in, e_w_pool, e_pool_scale, e_w_gate2, e_b_gate2, e_gla_norm_g, e_w_out,
           o_w_qkv, o_lam_q1, o_lam_k1, o_lam_q2, o_lam_k2, o_subln_g, o_w_out,
           ln1_g, ln1_b, ln2_g, ln2_b,
           moe_w_router, moe_b_router, moe_w_gate_up, moe_b_gate_up, moe_w_down, moe_b_down):
    depth = ln1_g.shape[0]
    alpha = (2.0 * depth) ** 0.25
    nb_p, seq, d = x_prompt.shape
    assert x_sample.shape[1:] == (seq, d)
    bsz = nb_p + x_sample.shape[0]
    n = bsz * seq
    x = [x_prompt.reshape(nb_p * seq, d), x_sample.reshape(n - nb_p * seq, d)]
    final_rows = tuple(p.shape[0] for p in x)

    n_exp = moe_w_router.shape[2]
    d_ff = moe_w_down.shape[2]
    w_gate, w_up = _deinterleave_bf16(moe_w_gate_up.reshape(depth * n_exp, d, 2 * d_ff))
    ew = (w_gate, w_up,
          moe_b_gate_up[..., 0::2].reshape(depth * n_exp, 1, d_ff),
          moe_b_gate_up[..., 1::2].reshape(depth * n_exp, 1, d_ff),
          _cast_bf16(moe_w_down.reshape(depth * n_exp, d_ff, d)),
          moe_b_down.reshape(depth * n_exp, 1, d))
    wr_t = jnp.swapaxes(moe_w_router, 1, 2).astype(BF16)
    tm_r = _tile(math.gcd(n, nb_p * seq), ROUTER_TILE)
    sub_r = _tile(tm_r, ROUTER_SUBTILE)
    tri = (jnp.arange(sub_r)[:, None] < jnp.arange(sub_r)[None, :]).astype(BF16)

    pool_width = e_w_pool.shape[1] * e_w_pool.shape[2]
    gla_width = d - pool_width
    keyw = gla_width // 2
    main_w = pool_width + 2 * keyw + 2 * gla_width
    hd = d // (2 * DIFF_HEADS)

    xs_buf = None
    for l in range(depth):
        i = l // 2
        if l % 2 == 0:
            w_in = e_w_in[i]
            h_main = _matmul(x, w_in[:, :main_w].astype(BF16), F32, "in_proj").reshape(bsz, seq, main_w)
            w_gl = jnp.pad(w_in[:, main_w:], ((0, 0), (0, LANES - 2 * GLA_GATE_RANK))).astype(BF16)
            h_gate = _matmul(x, w_gl, F32, "in_proj_gates").reshape(bsz, seq, LANES)
            a_out = _pool_mixer(h_main, e_w_pool[i].astype(BF16), e_pool_scale[i][None, :], pool_width)
            wg2 = e_w_gate2[i].astype(BF16)
            zpad = jnp.zeros((LANES - 2 * GLA_GATE_RANK, keyw), BF16)
            zrank = jnp.zeros((GLA_GATE_RANK, keyw), BF16)
            wgf = jnp.concatenate([wg2[0], zrank, zpad], axis=0)
            wgb = jnp.concatenate([zrank, wg2[1], zpad], axis=0)
            b_out = _gla_mixer(h_main, h_gate, wgf, wgb, e_b_gate2[i][0][None, :], e_b_gate2[i][1][None, :],
                               e_gla_norm_g[i][None, :], pool_width, GLA_HEADS)
            lhs = [a_out.reshape(n, pool_width), b_out.reshape(n, gla_width)]
            w_out = e_w_out[i].astype(BF16)
        else:
            lam_init = 0.8 - 0.6 * math.exp(-0.3 * l)
            cos, sin = _rope_tables(seq, hd)
            x_all = x[0] if len(x) == 1 else jnp.concatenate(x, axis=0)
            qkv = _qkv_proj(x_all, o_w_qkv[i].astype(BF16), cos, sin, seq, d, hd).reshape(bsz, seq, 3 * d)
            attn = _diff_attention(qkv, o_lam_q1[i][None, :], o_lam_k1[i][None, :], o_lam_q2[i][None, :],
                                   o_lam_k2[i][None, :], o_subln_g[i][None, :], lam_init, d, DIFF_HEADS)
            lhs = [attn.reshape(n, d)]
            w_out = o_w_out[i].astype(BF16)
        x1, idx, gates, rank, cnt = _proj_ln_router(
            lhs, w_out, x, ln1_g[l][None, :], ln1_b[l][None, :], wr_t[l], moe_b_router[l][:, None], tri,
            alpha, tm_r)
        out_rows = final_rows if l == depth - 1 else (n,)
        x, xs_buf = _moe_layer(x1, idx, gates, rank, cnt, l, ew, ln2_g[l][None, :], ln2_b[l][None, :], alpha,
                               xs_buf, out_rows)

    return (x[0].reshape(x_prompt.shape), x[1].reshape(x_sample.shape))
```
